```python
import math
import jax, jax.numpy as jnp
from jax import lax
import numpy as np

D_MODEL = 1024
BATCH = 16
SEQ = 2048
DEPTH = 4
DEC_BATCH = 2
DEC_SEQ = 16384
PAST_LEN = 128

N_MIXERS = 3
EXPAND = 2
D_INNER = EXPAND * D_MODEL
CONV_WIDTH = 31
CONV_HALF = CONV_WIDTH // 2
HGRN_HEAD_DIM = 128
HGRN_HEADS = D_INNER // HGRN_HEAD_DIM
HGRN_CHUNK = 32
ATTN_HEAD_DIM = 128
ATTN_HEADS = D_INNER // ATTN_HEAD_DIM
ATTN_KV_HEADS = 4
ATTN_GROUP = ATTN_HEADS // ATTN_KV_HEADS
KV_WIDTH = ATTN_KV_HEADS * ATTN_HEAD_DIM
ATTN_SCALE = ATTN_HEAD_DIM ** -0.5
WINDOW = 128
BLOCK = 128
REL_BUCKETS = 32
REL_MAX_DIST = 128
ALPHA = (2 * DEPTH) ** 0.25
BETA = (8 * DEPTH) ** -0.25
N_CONV_LAYERS = (DEPTH + 2) // 3
N_HGRN_LAYERS = (DEPTH + 1) // 3
N_ATTN_LAYERS = DEPTH // 3
LN_EPS = 1e-5

kernel_name = 'hybrid_conv_hgrn2_swa_encoder'


def layer_norm(x, g, b):
    xf = x.astype(jnp.float32)
    mu = xf.mean(-1, keepdims=True)
    var = jnp.square(xf - mu).mean(-1, keepdims=True)
    return ((xf - mu) * lax.rsqrt(var + LN_EPS) * g + b).astype(x.dtype)


def conv_mixer(x, w_in, conv_w, conv_b, ln_g, ln_b, w_out):
    h = x @ w_in
    a, b, g = jnp.split(h, 3, axis=-1)
    u = a * jax.nn.sigmoid(b)
    u = lax.conv_general_dilated(u, conv_w[:, None, :], window_strides=(1,),
                                 padding=[(CONV_HALF, CONV_HALF)],
                                 dimension_numbers=('NWC', 'WIO', 'NWC'),
                                 feature_group_count=D_INNER) + conv_b
    u = jax.nn.silu(layer_norm(u, ln_g, ln_b))
    return (u * jax.nn.silu(g)) @ w_out


def gla_chunk_scan(q, k, v, log_f):
    B, L, H, DK = q.shape
    DV = v.shape[-1]
    C = HGRN_CHUNK
    N = L // C
    def to_chunks(t):
        return t.reshape(B, N, C, H, t.shape[-1]).transpose(1, 0, 3, 2, 4)
    qc, kc, vc, fc = to_chunks(q), to_chunks(k), to_chunks(v), to_chunks(log_f)
    causal = jnp.tril(jnp.ones((C, C), dtype=bool))[:, :, None]
    def step(S, inp):
        q_, k_, v_, lf = inp
        bcum = jnp.cumsum(lf, axis=2)
        o_inter = jnp.einsum('bhtd,bhde->bhte', q_ * jnp.exp(bcum), S)
        rel = bcum[:, :, :, None, :] - bcum[:, :, None, :, :]
        decay = jnp.exp(jnp.where(causal, rel, -jnp.inf))
        A = jnp.einsum('bhtd,bhsd,bhtsd->bhts', q_, k_, decay)
        o_intra = jnp.einsum('bhts,bhse->bhte', A, v_)
        b_last = bcum[:, :, -1:, :]
        S_new = jnp.exp(b_last[:, :, 0, :])[..., None] * S + jnp.einsum(
            'bhsd,bhse->bhde', k_ * jnp.exp(b_last - bcum), v_)
        return S_new, o_inter + o_intra
    S0 = jnp.zeros((B, H, DK, DV), jnp.float32)
    _, o = lax.scan(step, S0, (qc, kc, vc, fc))
    return o.transpose(1, 0, 3, 2, 4).reshape(B, L, H, DV)


def hgrn_mixer(x, w_in, lb, norm_g, w_out):
    B, L, _ = x.shape
    h = (x @ w_in).astype(jnp.float32)
    q, f_fw, f_bw, i_in, g = jnp.split(h, 5, axis=-1)
    q = jax.nn.silu(q)
    def gates(raw, lbd):
        k = (1.0 - lbd) * jax.nn.sigmoid(-raw)
        log_f = jnp.logaddexp(jnp.log(lbd), jnp.log1p(-lbd) + jax.nn.log_sigmoid(raw))
        return k, log_f
    heads = lambda t: t.reshape(B, L, HGRN_HEADS, HGRN_HEAD_DIM)
    flip = lambda t: jnp.flip(t, axis=1)
    k_fw, lf_fw = gates(f_fw, lb[0])
    k_bw, lf_bw = gates(f_bw, lb[1])
    o_fw = gla_chunk_scan(heads(q), heads(k_fw), heads(i_in), heads(lf_fw))
    o_bw = flip(gla_chunk_scan(heads(flip(q)), heads(flip(k_bw)), heads(flip(i_in)), heads(flip(lf_bw))))
    o = o_fw + o_bw
    o = o * lax.rsqrt(jnp.mean(jnp.square(o), axis=-1, keepdims=True) + LN_EPS)
    o = o.reshape(B, L, D_INNER) * norm_g * jax.nn.silu(g)
    return o.astype(x.dtype) @ w_out


def t5_bucket(rel):
    nb = REL_BUCKETS // 2
    max_exact = nb // 2
    ret = (rel > 0).astype(jnp.int32) * nb
    n = jnp.abs(rel)
    large = max_exact + (jnp.log(jnp.maximum(n, 1).astype(jnp.float32) / max_exact)
                         / math.log(REL_MAX_DIST / max_exact) * (nb - max_exact)).astype(jnp.int32)
    large = jnp.minimum(large, nb - 1)
    return ret + jnp.where(n < max_exact, n, large)


def attn_mixer(x, w_in, sink, rel_bias, w_out):
    B, L, _ = x.shape
    N = L // BLOCK
    h = x @ w_in
    q = h[..., :D_INNER].reshape(B, N, BLOCK, ATTN_KV_HEADS, ATTN_GROUP, ATTN_HEAD_DIM)
    k = h[..., D_INNER:D_INNER + KV_WIDTH].reshape(B, L, ATTN_KV_HEADS, ATTN_HEAD_DIM)
    v = h[..., D_INNER + KV_WIDTH:D_INNER + 2 * KV_WIDTH].reshape(B, L, ATTN_KV_HEADS, ATTN_HEAD_DIM)
    g = h[..., D_INNER + 2 * KV_WIDTH:]
    pad = ((0, 0), (BLOCK, BLOCK), (0, 0), (0, 0))
    def windows(t):
        tp = jnp.pad(t, pad).reshape(B, N + 2, BLOCK, ATTN_KV_HEADS, ATTN_HEAD_DIM)
        return jnp.concatenate([tp[:, :-2], tp[:, 1:-1], tp[:, 2:]], axis=2)
    k_win, v_win = windows(k), windows(v)
    q_pos = jnp.arange(BLOCK)
    k_off = jnp.arange(3 * BLOCK) - BLOCK
    rel = k_off[None, :] - q_pos[:, None]
    bias = rel_bias[t5_bucket(rel)].astype(jnp.float32)
    bias = bias.transpose(2, 0, 1).reshape(ATTN_KV_HEADS, ATTN_GROUP, BLOCK, 3 * BLOCK)
    key_abs = jnp.arange(N)[:, None] * BLOCK + k_off[None, :]
    valid = ((jnp.abs(rel) <= WINDOW)[None]
             & ((key_abs >= 0) & (key_abs < L))[:, None, :])
    sink_l = sink.astype(jnp.float32).reshape(ATTN_KV_HEADS, ATTN_GROUP)[..., None, None]
    def block_attn(inp):
        qb, kb, vb, vm = inp
        s = jnp.einsum('bqkgd,bskd->bkgqs', qb, kb).astype(jnp.float32) * ATTN_SCALE + bias
        s = jnp.where(vm, s, -jnp.inf)
        m = jnp.maximum(s.max(-1, keepdims=True), sink_l)
        p = jnp.exp(s - m)
        p = p / (p.sum(-1, keepdims=True) + jnp.exp(sink_l - m))
        return jnp.einsum('bkgqs,bskd->bqkgd', p.astype(vb.dtype), vb)
    o = lax.map(block_attn, (q.transpose(1, 0, 2, 3, 4, 5), k_win.transpose(1, 0, 2, 3, 4),
                             v_win.transpose(1, 0, 2, 3, 4), valid))
    o = o.transpose(1, 0, 2, 3, 4, 5).reshape(B, L, D_INNER)
    return (o * jax.nn.silu(g)) @ w_out


def trunk(x, ln_g, ln_b, w_in_conv, conv_w, conv_b, conv_ln_g, conv_ln_b, w_out_conv,
          w_in_hgrn, lb_all, hgrn_norm_g, w_out_hgrn, w_in_attn, attn_sink, rel_bias, w_out_attn):
    for i in range(DEPTH):
        j = i // N_MIXERS
        kind = i % N_MIXERS
        if kind == 0:
            y = conv_mixer(x, w_in_conv[j], conv_w[j], conv_b[j], conv_ln_g[j], conv_ln_b[j], w_out_conv[j])
        elif kind == 1:
            y = hgrn_mixer(x, w_in_hgrn[j], lb_all[:, i], hgrn_norm_g[j], w_out_hgrn[j])
        else:
            y = attn_mixer(x, w_in_attn[j], attn_sink[j], rel_bias, w_out_attn[j])
        x = layer_norm(ALPHA * x + y, ln_g[i], ln_b[i])
    return x


def setup_inputs(seed: int = 0) -> dict:
    key = jax.random.key(seed)
    ks = jax.random.split(key, 18)
    def nrm(k, shape, s):
        return jax.random.normal(k, shape, jnp.float32) * s
    return {
        'x_prompt': nrm(ks[0], (BATCH, SEQ, D_MODEL), 1.0),
        'x_sample': nrm(ks[1], (DEC_BATCH, DEC_SEQ, D_MODEL), 1.0),
        'ln_g': 1.0 + nrm(ks[2], (DEPTH, D_MODEL), 0.02),
        'ln_b': nrm(ks[3], (DEPTH, D_MODEL), 0.02),
        'w_in_conv': nrm(ks[4], (N_CONV_LAYERS, D_MODEL, 3 * D_INNER), D_MODEL ** -0.5),
        'conv_w': nrm(ks[5], (N_CONV_LAYERS, CONV_WIDTH, D_INNER), CONV_WIDTH ** -0.5),
        'conv_b': nrm(ks[6], (N_CONV_LAYERS, D_INNER), 0.02),
        'conv_ln_g': 1.0 + nrm(ks[7], (N_CONV_LAYERS, D_INNER), 0.02),
        'conv_ln_b': nrm(ks[8], (N_CONV_LAYERS, D_INNER), 0.02),
        'w_out_conv': nrm(ks[9], (N_CONV_LAYERS, D_INNER, D_MODEL), BETA * D_INNER ** -0.5),
        'w_in_hgrn': nrm(ks[10], (N_HGRN_LAYERS, D_MODEL, 5 * D_INNER), D_MODEL ** -0.5),
        'hgrn_lb': nrm(ks[11], (2, DEPTH, D_INNER), 0.1),
        'hgrn_norm_g': 1.0 + nrm(ks[12], (N_HGRN_LAYERS, D_INNER), 0.02),
        'w_out_hgrn': nrm(ks[13], (N_HGRN_LAYERS, D_INNER, D_MODEL), BETA * D_INNER ** -0.5),
        'w_in_attn': nrm(ks[14], (N_ATTN_LAYERS, D_MODEL, 2 * D_INNER + 2 * KV_WIDTH), D_MODEL ** -0.5),
        'attn_sink': nrm(ks[15], (N_ATTN_LAYERS, ATTN_HEADS), 0.5),
        'rel_bias': nrm(ks[16], (REL_BUCKETS, ATTN_HEADS), 0.1),
        'w_out_attn': nrm(ks[17], (N_ATTN_LAYERS, D_INNER, D_MODEL), BETA * D_INNER ** -0.5),
    }


def reference(x_prompt, x_sample, ln_g, ln_b, w_in_conv, conv_w, conv_b, conv_ln_g, conv_ln_b,
              w_out_conv, w_in_hgrn, hgrn_lb, hgrn_norm_g, w_out_hgrn, w_in_attn, attn_sink,
              rel_bias, w_out_attn):
    p = jax.nn.softmax(hgrn_lb.astype(jnp.float32), axis=1)
    lb_all = jnp.cumsum(p, axis=1) - p[:, :1]
    y_prompt = trunk(x_prompt, ln_g, ln_b, w_in_conv, conv_w, conv_b, conv_ln_g, conv_ln_b, w_out_conv,
                     w_in_hgrn, lb_all, hgrn_norm_g, w_out_hgrn, w_in_attn, attn_sink, rel_bias, w_out_attn)
    y_sample = trunk(x_sample, ln_g, ln_b, w_in_conv, conv_w, conv_b, conv_ln_g, conv_ln_b, w_out_conv,
                     w_in_hgrn, lb_all, hgrn_norm_g, w_out_hgrn, w_in_attn, attn_sink, rel_bias, w_out_attn)
    return (y_prompt, y_sample)
```

```python
import functools
import math

import jax
import jax.numpy as jnp
from jax import lax
from jax.experimental import pallas as pl
from jax.experimental.pallas import tpu as pltpu

D_MODEL = 1024
DEPTH = 4
N_MIXERS = 3
D_INNER = 2 * D_MODEL
CONV_WIDTH = 31
CONV_HALF = CONV_WIDTH // 2
HEAD_DIM = 128
N_HEADS = D_INNER // HEAD_DIM
KV_HEADS = 4
GROUP = N_HEADS // KV_HEADS
KV_WIDTH = KV_HEADS * HEAD_DIM
ATTN_SCALE = HEAD_DIM ** -0.5
WINDOW = 128
BLOCK = 128
REL_BUCKETS = 32
REL_MAX_DIST = 128
ALPHA = (2 * DEPTH) ** 0.25
LN_EPS = 1e-5

LANES = 128
SUBLANES = 8
VMEM_LIMIT = 56 * 1024 * 1024

CONV_TL = 256
CONV_HALO = 16
CONV_CB = 512
CONV_RC = 128
SCAN_C = 128
PROJ_TM = 1024
PROJ_TN = 1024
OUT_TM = 256
F_FLOOR = 1e-37

F32 = jnp.float32
BF16 = jnp.bfloat16


def _const_spec(shape):
    return pl.BlockSpec(shape, lambda *_: (0,) * len(shape), pipeline_mode=pl.Buffered(1))


def _sigmoid(x):
    return 1.0 / (1.0 + jnp.exp(-x))


def _silu(x):
    return x * _sigmoid(x)


def _res_ln(x, y, g, b):
    z = ALPHA * x + y
    mu = jnp.mean(z, axis=-1, keepdims=True)
    d = z - mu
    var = jnp.mean(d * d, axis=-1, keepdims=True)
    return d * lax.rsqrt(var + LN_EPS) * g + b


def _proj_kernel(x_ref, w_ref, o_ref):
    o_ref[...] = jnp.dot(x_ref[...].astype(BF16), w_ref[...], preferred_element_type=F32)


def _in_proj(x2d, w_bf):
    t, d = x2d.shape
    n = w_bf.shape[1]
    return pl.pallas_call(
        _proj_kernel,
        grid=(t // PROJ_TM, n // PROJ_TN),
        in_specs=[pl.BlockSpec((PROJ_TM, d), lambda i, j: (i, 0)),
                  pl.BlockSpec((d, PROJ_TN), lambda i, j: (0, j))],
        out_specs=pl.BlockSpec((PROJ_TM, PROJ_TN), lambda i, j: (i, j)),
        out_shape=jax.ShapeDtypeStruct((t, n), F32),
        compiler_params=pltpu.CompilerParams(dimension_semantics=("arbitrary", "arbitrary"),
                                             vmem_limit_bytes=VMEM_LIMIT),
        name="in_proj",
    )(x2d, w_bf)


def _conv_layer_kernel(xp_ref, xc_ref, xn_ref, wab_ref, wg_ref, cw_ref, cb_ref, lng_ref, lnb_ref,
                       wout_ref, g2_ref, b2_ref, o_ref, uext_ref, conv_ref, sg_ref, v_ref):
    tl, halo = CONV_TL, CONV_HALO
    n_lane_blocks = D_INNER // LANES
    t = pl.program_id(1)
    nt = pl.num_programs(1)
    xc = xc_ref[0]
    xe = jnp.concatenate([xp_ref[0], xc, xn_ref[0]], axis=0).astype(BF16)
    xcb = xe[halo:halo + tl]

    rows = lax.broadcasted_iota(jnp.int32, (tl + 2 * halo, CONV_CB), 0)
    lo = jnp.where(t > 0, 0, halo)
    hi = jnp.where(t < nt - 1, tl + 2 * halo, tl + halo)
    inside = (rows >= lo) & (rows < hi)
    per_cb = CONV_CB // LANES
    for cb in range(D_INNER // CONV_CB):
        ab = jnp.dot(xe, wab_ref[cb], preferred_element_type=F32)
        u = ab[:, :CONV_CB] * _sigmoid(ab[:, CONV_CB:])
        u = jnp.where(inside, u, 0.0)
        for j in range(per_cb):
            uext_ref[cb * per_cb + j] = u[:, j * LANES:(j + 1) * LANES]
        g = jnp.dot(xcb, wg_ref[cb], preferred_element_type=F32)
        sg_ref[:, cb * CONV_CB:(cb + 1) * CONV_CB] = _silu(g)

    def lane_block(j, carry):
        w = cw_ref[j]
        bias = cb_ref[j]
        for rc in range(tl // CONV_RC):
            acc = jnp.broadcast_to(bias, (CONV_RC, LANES))
            for k in range(CONV_WIDTH):
                start = rc * CONV_RC + k + halo - CONV_HALF
                acc = acc + w[k:k + 1, :] * uext_ref[j, pl.ds(start, CONV_RC), :]
            conv_ref[j, pl.ds(rc * CONV_RC, CONV_RC), :] = acc
        return carry

    lax.fori_loop(0, n_lane_blocks, lane_block, 0)

    s1 = conv_ref[0]
    for j in range(1, n_lane_blocks):
        s1 = s1 + conv_ref[j]
    mu = jnp.sum(s1, axis=-1, keepdims=True) * (1.0 / D_INNER)
    s2 = jnp.zeros((tl, LANES), F32)
    for j in range(n_lane_blocks):
        d = conv_ref[j] - mu
        s2 = s2 + d * d
    rstd = lax.rsqrt(jnp.sum(s2, axis=-1, keepdims=True) * (1.0 / D_INNER) + LN_EPS)
    for j in range(n_lane_blocks):
        sl = slice(j * LANES, (j + 1) * LANES)
        c = (conv_ref[j] - mu) * rstd * lng_ref[:, sl] + lnb_ref[:, sl]
        v_ref[:, sl] = (_silu(c) * sg_ref[:, sl]).astype(BF16)

    y = jnp.dot(v_ref[...], wout_ref[...], preferred_element_type=F32)
    o_ref[0] = _res_ln(xc, y, g2_ref[...], b2_ref[...])


def _conv_layer(x, w_in, conv_w, conv_b, ln_g, ln_b, w_out, g2, b2):
    bsz, seq, d = x.shape
    tl, halo = CONV_TL, CONV_HALO
    n_cb = D_INNER // CONV_CB
    n_lb = D_INNER // LANES
    a, b, g = jnp.split(w_in.astype(BF16), 3, axis=-1)
    wab = jnp.concatenate([a.reshape(d, n_cb, CONV_CB), b.reshape(d, n_cb, CONV_CB)], axis=-1)
    wab = wab.transpose(1, 0, 2)
    wg = g.reshape(d, n_cb, CONV_CB).transpose(1, 0, 2)
    cw = jnp.pad(conv_w, ((0, 1), (0, 0))).reshape(CONV_WIDTH + 1, n_lb, LANES).transpose(1, 0, 2)
    cb = conv_b.reshape(n_lb, 1, LANES)
    hpt = tl // halo
    last_halo = seq // halo - 1
    return pl.pallas_call(
        _conv_layer_kernel,
        grid=(bsz, seq // tl),
        in_specs=[
            pl.BlockSpec((1, halo, d), lambda i, t: (i, jnp.maximum(t * hpt - 1, 0), 0)),
            pl.BlockSpec((1, tl, d), lambda i, t: (i, t, 0)),
            pl.BlockSpec((1, halo, d), lambda i, t: (i, jnp.minimum((t + 1) * hpt, last_halo), 0)),
            _const_spec((n_cb, d, 2 * CONV_CB)),
            _const_spec((n_cb, d, CONV_CB)),
            _const_spec((n_lb, CONV_WIDTH + 1, LANES)),
            _const_spec((n_lb, 1, LANES)),
            _const_spec((1, D_INNER)),
            _const_spec((1, D_INNER)),
            _const_spec((D_INNER, d)),
            _const_spec((1, d)),
            _const_spec((1, d)),
        ],
        out_specs=pl.BlockSpec((1, tl, d), lambda i, t: (i, t, 0)),
        out_shape=jax.ShapeDtypeStruct((bsz, seq, d), F32),
        scratch_shapes=[
            pltpu.VMEM((n_lb, tl + 2 * halo, LANES), F32),
            pltpu.VMEM((n_lb, tl, LANES), F32),
            pltpu.VMEM((tl, D_INNER), F32),
            pltpu.VMEM((tl, D_INNER), BF16),
        ],
        compiler_params=pltpu.CompilerParams(dimension_semantics=("arbitrary", "arbitrary"),
                                             vmem_limit_bytes=VMEM_LIMIT),
        name="conv_layer",
    )(x, x, x, wab, wg, cw, cb, ln_g.reshape(1, -1), ln_b.reshape(1, -1), w_out.astype(BF16),
      g2.reshape(1, -1), b2.reshape(1, -1))


def _block_ref_rows(p, m, src):
    c = p.shape[0]
    if 2 * m >= SUBLANES:
        p3 = p.reshape(c // (2 * m), 2 * m, LANES)
        return jnp.broadcast_to(p3[:, src:src + 1, :], p3.shape).reshape(c, LANES)
    pos = lax.broadcasted_iota(jnp.int32, p.shape, 0) & (2 * m - 1)
    e = p
    for r in range(2 * m):
        if r != src:
            e = jnp.where(pos == r, pltpu.roll(p, (r - src) % c, 0), e)
    return e


def _scan_chunk(q, raw, v, lb, s_ref, reverse):
    c = q.shape[0]
    e_abs = jnp.exp(-jnp.abs(raw))
    r_abs = 1.0 / (1.0 + e_abs)
    pos_raw = raw >= 0
    sig = jnp.where(pos_raw, r_abs, e_abs * r_abs)
    sig_neg = jnp.where(pos_raw, e_abs * r_abs, r_abs)
    k = (1.0 - lb) * sig_neg
    p = jnp.log(jnp.maximum(lb + (1.0 - lb) * sig, F_FLOOR))
    q = _silu(q)

    row = lax.broadcasted_iota(jnp.int32, (c, LANES), 0)
    tt = lax.broadcasted_iota(jnp.int32, (c, c), 0)
    ss = lax.broadcasted_iota(jnp.int32, (c, c), 1)
    causal = (tt < ss) if reverse else (tt > ss)
    xor = tt ^ ss
    a = jnp.zeros((c, c), F32)
    n_levels = c.bit_length() - 1
    for lvl in range(n_levels):
        m = 1 << lvl
        q_side = ((row & m) == 0) if reverse else ((row & m) != 0)
        e = _block_ref_rows(p, m, m if reverse else m - 1)
        w = jnp.exp(jnp.where(q_side, p, e - p))
        al = lax.dot_general((q * w).astype(BF16), (k * w).astype(BF16),
                             (((1,), (1,)), ((), ())), preferred_element_type=F32)
        a = a + jnp.where(causal & ((xor >> lvl) == 1), al, 0.0)
        p = jnp.where(q_side, p + e, p)

    edge = p[0:1, :] if reverse else p[c - 1:c, :]
    st = s_ref[...]
    o = lax.dot_general((q * jnp.exp(p)).astype(BF16), st.astype(BF16),
                        (((1,), (1,)), ((), ())), preferred_element_type=F32)
    vb = v.astype(BF16)
    o = o + jnp.dot(a.astype(BF16), vb, preferred_element_type=F32)
    o = o + jnp.sum(q * k, axis=-1, keepdims=True) * v
    k_dec = (k * jnp.exp(edge - p)).astype(BF16)
    s_ref[...] = jnp.exp(edge) * st + lax.dot_general(
        vb, k_dec, (((0,), (0,)), ((), ())), preferred_element_type=F32)
    return o


def _hgrn_scan_kernel(layer, qf_ref, ff_ref, if_ref, qb_ref, fb_ref, ib_ref, lb_ref,
                      of_ref, ob_ref, s_ref):
    @pl.when(pl.program_id(2) == 0)
    def _():
        s_ref[...] = jnp.zeros_like(s_ref)

    z = lb_ref[...]
    ez = jnp.exp(z - jnp.max(z, axis=1, keepdims=True))
    pz = ez / jnp.sum(ez, axis=1, keepdims=True)
    lb = jnp.sum(pz[:, :layer + 1, :], axis=1) - pz[:, 0, :]

    of_ref[0] = _scan_chunk(qf_ref[0], ff_ref[0], if_ref[0], lb[0:1, :], s_ref.at[0], False)
    ob_ref[0] = _scan_chunk(qb_ref[0], fb_ref[0], ib_ref[0], lb[1:2, :], s_ref.at[1], True)


def _hgrn_scan(h, hgrn_lb, layer):
    bsz, seq, _ = h.shape
    c = SCAN_C
    nc = seq // c

    def col(part, rev):
        if rev:
            return pl.BlockSpec((1, c, LANES), lambda i, hd, n: (i, nc - 1 - n, part * N_HEADS + hd))
        return pl.BlockSpec((1, c, LANES), lambda i, hd, n: (i, n, part * N_HEADS + hd))

    out_f = pl.BlockSpec((1, c, LANES), lambda i, hd, n: (i, n, hd))
    out_b = pl.BlockSpec((1, c, LANES), lambda i, hd, n: (i, nc - 1 - n, hd))
    return pl.pallas_call(
        functools.partial(_hgrn_scan_kernel, layer),
        grid=(bsz, N_HEADS, nc),
        in_specs=[col(0, False), col(1, False), col(3, False),
                  col(0, True), col(2, True), col(3, True),
                  pl.BlockSpec((2, DEPTH, LANES), lambda i, hd, n: (0, 0, hd))],
        out_specs=[out_f, out_b],
        out_shape=[jax.ShapeDtypeStruct((bsz, seq, D_INNER), F32)] * 2,
        scratch_shapes=[pltpu.VMEM((2, HEAD_DIM, HEAD_DIM), F32)],
        compiler_params=pltpu.CompilerParams(
            dimension_semantics=("arbitrary", "arbitrary", "arbitrary"), vmem_limit_bytes=VMEM_LIMIT),
        name="hgrn_scan",
    )(h, h, h, h, h, h, hgrn_lb)


def _hgrn_out_kernel(of_ref, ob_ref, g_ref, x_ref, ng_ref, wout_ref, g2_ref, b2_ref, o_ref, v_ref):
    for hd in range(N_HEADS):
        sl = slice(hd * HEAD_DIM, (hd + 1) * HEAD_DIM)
        o = of_ref[:, sl] + ob_ref[:, sl]
        o = o * lax.rsqrt(jnp.mean(o * o, axis=-1, keepdims=True) + LN_EPS)
        v_ref[:, sl] = (o * ng_ref[:, sl] * _silu(g_ref[:, sl])).astype(BF16)
    y = jnp.dot(v_ref[...], wout_ref[...], preferred_element_type=F32)
    o_ref[...] = _res_ln(x_ref[...], y, g2_ref[...], b2_ref[...])


def _hgrn_out(o_fw, o_bw, h2d, x2d, norm_g, w_out, g2, b2):
    t, d = x2d.shape
    tm = OUT_TM
    row = lambda i: (i, 0)
    return pl.pallas_call(
        _hgrn_out_kernel,
        grid=(t // tm,),
        in_specs=[pl.BlockSpec((tm, D_INNER), row), pl.BlockSpec((tm, D_INNER), row),
                  pl.BlockSpec((tm, D_INNER), lambda i: (i, 4)),
                  pl.BlockSpec((tm, d), row),
                  _const_spec((1, D_INNER)), _const_spec((D_INNER, d)),
                  _const_spec((1, d)), _const_spec((1, d))],
        out_specs=pl.BlockSpec((tm, d), row),
        out_shape=jax.ShapeDtypeStruct((t, d), F32),
        scratch_shapes=[pltpu.VMEM((tm, D_INNER), BF16)],
        compiler_params=pltpu.CompilerParams(dimension_semantics=("arbitrary",),
                                             vmem_limit_bytes=VMEM_LIMIT),
        name="hgrn_out",
    )(o_fw, o_bw, h2d, x2d, norm_g.reshape(1, -1), w_out.astype(BF16), g2.reshape(1, -1),
      b2.reshape(1, -1))


def _hgrn_layer(x, w_in, hgrn_lb, layer, norm_g, w_out, g2, b2):
    bsz, seq, d = x.shape
    x2d = x.reshape(bsz * seq, d)
    h2d = _in_proj(x2d, w_in.astype(BF16))
    o_fw, o_bw = _hgrn_scan(h2d.reshape(bsz, seq, -1), hgrn_lb, layer)
    y = _hgrn_out(o_fw.reshape(bsz * seq, -1), o_bw.reshape(bsz * seq, -1), h2d, x2d,
                  norm_g, w_out, g2, b2)
    return y.reshape(bsz, seq, d)


def _t5_bucket(rel):
    nb = REL_BUCKETS // 2
    max_exact = nb // 2
    ret = (rel > 0).astype(jnp.int32) * nb
    n = jnp.abs(rel)
    large = max_exact + (jnp.log(jnp.maximum(n, 1).astype(jnp.float32) / max_exact)
                         / math.log(REL_MAX_DIST / max_exact) * (nb - max_exact)).astype(jnp.int32)
    large = jnp.minimum(large, nb - 1)
    return ret + jnp.where(n < max_exact, n, large)


def _attn_layer_kernel(seq, q_ref, kp_ref, kc_ref, kn_ref, vp_ref, vc_ref, vn_ref, g_ref, x_ref,
                       bucket_ref, relb_ref, sink_ref, wout_ref, g2_ref, b2_ref, o_ref,
                       bias_ref, v_ref):
    n = pl.program_id(1)

    @pl.when((pl.program_id(0) == 0) & (n == 0))
    def _():
        bucket = bucket_ref[...]
        for h in range(N_HEADS):
            acc = jnp.zeros(bucket.shape, F32)
            for b in range(REL_BUCKETS):
                acc = jnp.where(bucket == b, relb_ref[b, h], acc)
            bias_ref[h] = acc

    kwin = jnp.concatenate([kp_ref[0], kc_ref[0], kn_ref[0]], axis=0).astype(BF16)
    vwin = jnp.concatenate([vp_ref[0], vc_ref[0], vn_ref[0]], axis=0).astype(BF16)
    qb = q_ref[0].astype(BF16)

    rows = GROUP * BLOCK
    col = lax.broadcasted_iota(jnp.int32, (rows, 3 * BLOCK), 1)
    qpos = lax.broadcasted_iota(jnp.int32, (rows, 3 * BLOCK), 0) & (BLOCK - 1)
    rel = col - BLOCK - qpos
    key_abs = n * BLOCK + col - BLOCK
    valid = (jnp.abs(rel) <= WINDOW) & (key_abs >= 0) & (key_abs < seq)
    head_of_row = lax.broadcasted_iota(jnp.int32, (rows, 1), 0) // BLOCK

    for kh in range(KV_HEADS):
        ksl = slice(kh * HEAD_DIM, (kh + 1) * HEAD_DIM)
        h0 = kh * GROUP
        qg = jnp.concatenate([qb[:, (h0 + gi) * HEAD_DIM:(h0 + gi + 1) * HEAD_DIM]
                              for gi in range(GROUP)], axis=0)
        s = lax.dot_general(qg, kwin[:, ksl], (((1,), (1,)), ((), ())),
                            preferred_element_type=F32)
        bias = bias_ref[pl.ds(h0, GROUP)].reshape(rows, 3 * BLOCK)
        s = jnp.where(valid, s * ATTN_SCALE + bias, -jnp.inf)
        sink = jnp.zeros((rows, 1), F32)
        for gi in range(GROUP):
            sink = jnp.where(head_of_row == gi, sink_ref[h0 + gi], sink)
        mx = jnp.maximum(jnp.max(s, axis=-1, keepdims=True), sink)
        p = jnp.exp(s - mx)
        den = jnp.sum(p, axis=-1, keepdims=True) + jnp.exp(sink - mx)
        o = jnp.dot(p.astype(BF16), vwin[:, ksl], preferred_element_type=F32) / den
        for gi in range(GROUP):
            sl = slice((h0 + gi) * HEAD_DIM, (h0 + gi + 1) * HEAD_DIM)
            v_ref[:, sl] = (o[gi * BLOCK:(gi + 1) * BLOCK] * _silu(g_ref[0, :, sl])).astype(BF16)

    y = jnp.dot(v_ref[...], wout_ref[...], preferred_element_type=F32)
    o_ref[0] = _res_ln(x_ref[0], y, g2_ref[...], b2_ref[...])


def _attn_layer(x, w_in, sink, rel_bias, w_out, g2, b2):
    bsz, seq, d = x.shape
    nb = seq // BLOCK
    wq, wk, wv, wg = jnp.split(w_in.astype(BF16), [D_INNER, D_INNER + KV_WIDTH, D_INNER + 2 * KV_WIDTH],
                               axis=-1)
    w_perm = jnp.concatenate([wq, wg, wk, wv], axis=-1)
    h = _in_proj(x.reshape(bsz * seq, d), w_perm).reshape(bsz, seq, -1)
    q_pos = jnp.arange(BLOCK)
    k_off = jnp.arange(3 * BLOCK) - BLOCK
    bucket = _t5_bucket(k_off[None, :] - q_pos[:, None])
    k_col = 2 * D_INNER // KV_WIDTH
    v_col = k_col + 1
    prev = lambda i, n: jnp.maximum(n - 1, 0)
    nxt = lambda i, n: jnp.minimum(n + 1, nb - 1)
    kv = lambda colb, f: pl.BlockSpec((1, BLOCK, KV_WIDTH), lambda i, n: (i, f(i, n), colb))
    cur = lambda i, n: n
    return pl.pallas_call(
        functools.partial(_attn_layer_kernel, seq),
        grid=(bsz, nb),
        in_specs=[
            pl.BlockSpec((1, BLOCK, D_INNER), lambda i, n: (i, n, 0)),
            kv(k_col, prev), kv(k_col, cur), kv(k_col, nxt),
            kv(v_col, prev), kv(v_col, cur), kv(v_col, nxt),
            pl.BlockSpec((1, BLOCK, D_INNER), lambda i, n: (i, n, 1)),
            pl.BlockSpec((1, BLOCK, d), lambda i, n: (i, n, 0)),
            _const_spec((BLOCK, 3 * BLOCK)),
            pl.BlockSpec(memory_space=pltpu.SMEM),
            pl.BlockSpec(memory_space=pltpu.SMEM),
            _const_spec((D_INNER, d)),
            _const_spec((1, d)),
            _const_spec((1, d)),
        ],
        out_specs=pl.BlockSpec((1, BLOCK, d), lambda i, n: (i, n, 0)),
        out_shape=jax.ShapeDtypeStruct((bsz, seq, d), F32),
        scratch_shapes=[pltpu.VMEM((N_HEADS, BLOCK, 3 * BLOCK), F32),
                        pltpu.VMEM((BLOCK, D_INNER), BF16)],
        compiler_params=pltpu.CompilerParams(dimension_semantics=("arbitrary", "arbitrary"),
                                             vmem_limit_bytes=VMEM_LIMIT),
        name="attn_layer",
    )(h, h, h, h, h, h, h, h, x, bucket, rel_bias, sink, w_out.astype(BF16),
      g2.reshape(1, -1), b2.reshape(1, -1))


def _trunk(x, ln_g, ln_b, w_in_conv, conv_w, conv_b, conv_ln_g, conv_ln_b, w_out_conv,
           w_in_hgrn, hgrn_lb, hgrn_norm_g, w_out_hgrn, w_in_attn, attn_sink, rel_bias, w_out_attn):
    for i in range(DEPTH):
        j = i // N_MIXERS
        kind = i % N_MIXERS
        if kind == 0:
            x = _conv_layer(x, w_in_conv[j], conv_w[j], conv_b[j], conv_ln_g[j], conv_ln_b[j],
                            w_out_conv[j], ln_g[i], ln_b[i])
        elif kind == 1:
            x = _hgrn_layer(x, w_in_hgrn[j], hgrn_lb, i, hgrn_norm_g[j], w_out_hgrn[j],
                            ln_g[i], ln_b[i])
        else:
            x = _attn_layer(x, w_in_attn[j], attn_sink[j], rel_bias, w_out_attn[j], ln_g[i], ln_b[i])
    return x


def kernel(x_prompt, x_sample, ln_g, ln_b, w_in_conv, conv_w, conv_b, conv_ln_g, conv_ln_b,
           w_out_conv, w_in_hgrn, hgrn_lb, hgrn_norm_g, w_out_hgrn, w_in_attn, attn_sink,
           rel_bias, w_out_attn):
    params = (ln_g, ln_b, w_in_conv, conv_w, conv_b, conv_ln_g, conv_ln_b, w_out_conv,
              w_in_hgrn, hgrn_lb, hgrn_norm_g, w_out_hgrn, w_in_attn, attn_sink, rel_bias, w_out_attn)
    return (_trunk(x_prompt, *params), _trunk(x_sample, *params))
```

```python
import functools
import math

import jax
import jax.numpy as jnp
import numpy as np
from jax import lax
from jax.experimental import pallas as pl
from jax.experimental.pallas import tpu as pltpu

D_MODEL = 1024
DEPTH = 4
N_MIXERS = 3
D_INNER = 2 * D_MODEL
CONV_WIDTH = 31
CONV_HALF = CONV_WIDTH // 2
HEAD_DIM = 128
N_HEADS = D_INNER // HEAD_DIM
KV_HEADS = 4
GROUP = N_HEADS // KV_HEADS
KV_WIDTH = KV_HEADS * HEAD_DIM
ATTN_SCALE = HEAD_DIM ** -0.5
WINDOW = 128
BLOCK = 128
REL_BUCKETS = 32
REL_MAX_DIST = 128
ALPHA = (2 * DEPTH) ** 0.25
LN_EPS = 1e-5

LANES = 128
SUBLANES = 8
VMEM_LIMIT = 56 * 1024 * 1024

CONV_TL = 256
CONV_HALO = 16
CONV_CB = 512
CONV_RC = 128
SCAN_C = 128
SCAN_LEVELS = SCAN_C.bit_length() - 1
SCAN_HB = 2
SCAN_CH = 2
SCAN_PHASES = 6
PROJ_TM = 1024
PROJ_TN = 1024
OUT_TM = 256
F_FLOOR = 1e-37

F32 = jnp.float32
BF16 = jnp.bfloat16


def _const_spec(shape):
    return pl.BlockSpec(shape, lambda *_: (0,) * len(shape), pipeline_mode=pl.Buffered(1))


def _sigmoid(x):
    return 1.0 / (1.0 + jnp.exp(-x))


def _silu(x):
    return x * _sigmoid(x)


def _res_ln(x, y, g, b):
    z = ALPHA * x + y
    mu = jnp.mean(z, axis=-1, keepdims=True)
    d = z - mu
    var = jnp.mean(d * d, axis=-1, keepdims=True)
    return d * lax.rsqrt(var + LN_EPS) * g + b


def _proj_kernel(x_ref, w_ref, o_ref):
    o_ref[...] = jnp.dot(x_ref[...].astype(BF16), w_ref[...],
                         preferred_element_type=F32).astype(o_ref.dtype)


def _in_proj(x2d, w_bf, out_dtype):
    t, d = x2d.shape
    n = w_bf.shape[1]
    return pl.pallas_call(
        _proj_kernel,
        grid=(t // PROJ_TM, n // PROJ_TN),
        in_specs=[pl.BlockSpec((PROJ_TM, d), lambda i, j: (i, 0)),
                  pl.BlockSpec((d, PROJ_TN), lambda i, j: (0, j))],
        out_specs=pl.BlockSpec((PROJ_TM, PROJ_TN), lambda i, j: (i, j)),
        out_shape=jax.ShapeDtypeStruct((t, n), out_dtype),
        compiler_params=pltpu.CompilerParams(dimension_semantics=("arbitrary", "arbitrary"),
                                             vmem_limit_bytes=VMEM_LIMIT),
        name="in_proj",
    )(x2d, w_bf)


def _conv_layer_kernel(xp_ref, xc_ref, xn_ref, wab_ref, wg_ref, cw_ref, cb_ref, lng_ref, lnb_ref,
                       wout_ref, g2_ref, b2_ref, o_ref, uext_ref, conv_ref, sg_ref, v_ref):
    tl, halo = CONV_TL, CONV_HALO
    n_lane_blocks = D_INNER // LANES
    t = pl.program_id(1)
    nt = pl.num_programs(1)
    xc = xc_ref[0]
    xe = jnp.concatenate([xp_ref[0], xc, xn_ref[0]], axis=0).astype(BF16)
    xcb = xe[halo:halo + tl]

    rows = lax.broadcasted_iota(jnp.int32, (tl + 2 * halo, CONV_CB), 0)
    lo = jnp.where(t > 0, 0, halo)
    hi = jnp.where(t < nt - 1, tl + 2 * halo, tl + halo)
    inside = (rows >= lo) & (rows < hi)
    per_cb = CONV_CB // LANES
    for cb in range(D_INNER // CONV_CB):
        ab = jnp.dot(xe, wab_ref[cb], preferred_element_type=F32)
        u = ab[:, :CONV_CB] * _sigmoid(ab[:, CONV_CB:])
        u = jnp.where(inside, u, 0.0)
        for j in range(per_cb):
            uext_ref[cb * per_cb + j] = u[:, j * LANES:(j + 1) * LANES]
        g = jnp.dot(xcb, wg_ref[cb], preferred_element_type=F32)
        sg_ref[:, cb * CONV_CB:(cb + 1) * CONV_CB] = _silu(g)

    def lane_block(j, carry):
        w = cw_ref[j]
        bias = cb_ref[j]
        for rc in range(tl // CONV_RC):
            acc = jnp.broadcast_to(bias, (CONV_RC, LANES))
            for k in range(CONV_WIDTH):
                start = rc * CONV_RC + k + halo - CONV_HALF
                acc = acc + w[k:k + 1, :] * uext_ref[j, pl.ds(start, CONV_RC), :]
            conv_ref[j, pl.ds(rc * CONV_RC, CONV_RC), :] = acc
        return carry

    lax.fori_loop(0, n_lane_blocks, lane_block, 0)

    s1 = conv_ref[0]
    for j in range(1, n_lane_blocks):
        s1 = s1 + conv_ref[j]
    mu = jnp.sum(s1, axis=-1, keepdims=True) * (1.0 / D_INNER)
    s2 = jnp.zeros((tl, LANES), F32)
    for j in range(n_lane_blocks):
        d = conv_ref[j] - mu
        s2 = s2 + d * d
    rstd = lax.rsqrt(jnp.sum(s2, axis=-1, keepdims=True) * (1.0 / D_INNER) + LN_EPS)
    for j in range(n_lane_blocks):
        sl = slice(j * LANES, (j + 1) * LANES)
        c = (conv_ref[j] - mu) * rstd * lng_ref[:, sl] + lnb_ref[:, sl]
        v_ref[:, sl] = (_silu(c) * sg_ref[:, sl]).astype(BF16)

    y = jnp.dot(v_ref[...], wout_ref[...], preferred_element_type=F32)
    o_ref[0] = _res_ln(xc, y, g2_ref[...], b2_ref[...])


def _conv_layer(x, w_in, conv_w, conv_b, ln_g, ln_b, w_out, g2, b2):
    bsz, seq, d = x.shape
    tl, halo = CONV_TL, CONV_HALO
    n_cb = D_INNER // CONV_CB
    n_lb = D_INNER // LANES
    a, b, g = jnp.split(w_in.astype(BF16), 3, axis=-1)
    wab = jnp.concatenate([a.reshape(d, n_cb, CONV_CB), b.reshape(d, n_cb, CONV_CB)], axis=-1)
    wab = wab.transpose(1, 0, 2)
    wg = g.reshape(d, n_cb, CONV_CB).transpose(1, 0, 2)
    cw = jnp.pad(conv_w, ((0, 1), (0, 0))).reshape(CONV_WIDTH + 1, n_lb, LANES).transpose(1, 0, 2)
    cb = conv_b.reshape(n_lb, 1, LANES)
    hpt = tl // halo
    last_halo = seq // halo - 1
    return pl.pallas_call(
        _conv_layer_kernel,
        grid=(bsz, seq // tl),
        in_specs=[
            pl.BlockSpec((1, halo, d), lambda i, t: (i, jnp.maximum(t * hpt - 1, 0), 0)),
            pl.BlockSpec((1, tl, d), lambda i, t: (i, t, 0)),
            pl.BlockSpec((1, halo, d), lambda i, t: (i, jnp.minimum((t + 1) * hpt, last_halo), 0)),
            _const_spec((n_cb, d, 2 * CONV_CB)),
            _const_spec((n_cb, d, CONV_CB)),
            _const_spec((n_lb, CONV_WIDTH + 1, LANES)),
            _const_spec((n_lb, 1, LANES)),
            _const_spec((1, D_INNER)),
            _const_spec((1, D_INNER)),
            _const_spec((D_INNER, d)),
            _const_spec((1, d)),
            _const_spec((1, d)),
        ],
        out_specs=pl.BlockSpec((1, tl, d), lambda i, t: (i, t, 0)),
        out_shape=jax.ShapeDtypeStruct((bsz, seq, d), F32),
        scratch_shapes=[
            pltpu.VMEM((n_lb, tl + 2 * halo, LANES), F32),
            pltpu.VMEM((n_lb, tl, LANES), F32),
            pltpu.VMEM((tl, D_INNER), F32),
            pltpu.VMEM((tl, D_INNER), BF16),
        ],
        compiler_params=pltpu.CompilerParams(dimension_semantics=("arbitrary", "arbitrary"),
                                             vmem_limit_bytes=VMEM_LIMIT),
        name="conv_layer",
    )(x, x, x, wab, wg, cw, cb, ln_g.reshape(1, -1), ln_b.reshape(1, -1), w_out.astype(BF16),
      g2.reshape(1, -1), b2.reshape(1, -1))


def _scan_masks():
    half, nb8 = SCAN_C // 2, SCAN_C // SUBLANES
    i = np.arange(half)[:, None]
    j = np.arange(half)[None, :]
    same8 = (i % nb8) == (j % nb8)
    masks = [same8 & (i // (nb8 << lvl) == j // (nb8 << lvl)) for lvl in range(3)]
    masks += [i // (SUBLANES << lvl) == j // (SUBLANES << lvl) for lvl in range(SCAN_LEVELS - 3)]
    return np.stack(masks).astype(np.float32)


def _gates(raw, lb):
    e_abs = jnp.exp(-jnp.abs(raw))
    r_abs = 1.0 / (1.0 + e_abs)
    er = e_abs * r_abs
    pos = raw >= 0
    c1 = 1.0 - lb
    k = c1 * jnp.where(pos, er, r_abs)
    f = lb + c1 * jnp.where(pos, r_abs, er)
    return k, jnp.log(jnp.maximum(f, F_FLOOR))


_NT = (((1,), (1,)), ((), ()))


def _level_scores(qc, kc, mask):
    s = lax.dot_general(qc.astype(BF16), kc.astype(BF16), _NT, preferred_element_type=F32)
    if mask is not None:
        s = s * mask
    return s.astype(BF16)


def _scan_chunk(q_ref, f_ref, v_ref, row0, lb, st_ref, tmp_ref, masks_ref, reverse):
    c, sub = SCAN_C, SUBLANES
    nb8 = c // sub

    def piece(ref, r):
        return ref[0, pl.ds(row0 + r, nb8, stride=sub), :]

    q = [_silu(piece(q_ref, r)) for r in range(sub)]
    kp = [_gates(piece(f_ref, r), lb) for r in range(sub)]
    k = [a for a, _ in kp]
    p = [b for _, b in kp]
    v = [piece(v_ref, r) for r in range(sub)]
    o = [jnp.sum(q[r] * k[r], axis=-1, keepdims=True) * v[r] for r in range(sub)]
    yield None
    low = []
    for lvl in range(3):
        m = 1 << lvl
        q_rows, k_rows, e_of = [], [], {}
        for blk in range(0, sub, 2 * m):
            lo, hi = list(range(blk, blk + m)), list(range(blk + m, blk + 2 * m))
            q_rows += lo if reverse else hi
            k_rows += hi if reverse else lo
            for r in lo + hi:
                e_of[r] = blk + m if reverse else blk + m - 1
        qc = jnp.concatenate([q[r] * jnp.exp(p[r]) for r in q_rows], axis=0)
        kc = jnp.concatenate([k[r] if r == e_of[r] else k[r] * jnp.exp(p[e_of[r]] - p[r])
                              for r in k_rows], axis=0)
        vc = jnp.concatenate([v[r] for r in k_rows], axis=0)
        low.append((q_rows, _level_scores(qc, kc, masks_ref[lvl]), vc.astype(BF16)))
        for r in q_rows:
            p[r] = p[r] + p[e_of[r]]
    for r in range(sub):
        for slot, arr in enumerate((p, q, k)):
            tmp_ref[slot, pl.ds(r, nb8, stride=sub), :] = arr[r]
    yield None
    low = [(q_rows, jnp.dot(sc, vc, preferred_element_type=F32)) for q_rows, sc, vc in low]
    yield None
    for q_rows, ol in low:
        for i, r in enumerate(q_rows):
            o[r] = o[r] + ol[i * nb8:(i + 1) * nb8]
    for r in range(sub):
        tmp_ref[3, pl.ds(r, nb8, stride=sub), :] = o[r]
    tile = lambda x, i: x[i * sub:(i + 1) * sub]
    pn, qn, kn, vn = tmp_ref[0], tmp_ref[1], tmp_ref[2], v_ref[0, row0:row0 + c, :]
    pt = [tile(pn, i) for i in range(nb8)]
    qt = [tile(qn, i) for i in range(nb8)]
    kt = [tile(kn, i) for i in range(nb8)]
    vt = [tile(vn, i) for i in range(nb8)]
    high = []
    for lvl in range(3, SCAN_LEVELS):
        mt = (1 << lvl) // sub
        q_tiles, k_tiles, eb = [], [], {}
        for blk in range(0, nb8, 2 * mt):
            lo, hi = list(range(blk, blk + mt)), list(range(blk + mt, blk + 2 * mt))
            q_tiles += lo if reverse else hi
            k_tiles += hi if reverse else lo
            e = pt[blk + mt][0:1, :] if reverse else pt[blk + mt - 1][sub - 1:sub, :]
            e = jnp.broadcast_to(e, (sub, LANES))
            for i in lo + hi:
                eb[i] = e
        qc = jnp.concatenate([qt[i] * jnp.exp(pt[i]) for i in q_tiles], axis=0)
        kc = jnp.concatenate([kt[i] * jnp.exp(eb[i] - pt[i]) for i in k_tiles], axis=0)
        vc = jnp.concatenate([vt[i] for i in k_tiles], axis=0)
        high.append((q_tiles, _level_scores(qc, kc, masks_ref[lvl] if lvl < SCAN_LEVELS - 1 else None),
                     vc.astype(BF16)))
        for i in q_tiles:
            pt[i] = pt[i] + eb[i]
    yield None
    pn = jnp.concatenate(pt, axis=0)
    edge = pt[0][0:1, :] if reverse else pt[nb8 - 1][sub - 1:sub, :]
    st = st_ref[...]
    o_inter = lax.dot_general((qn * jnp.exp(pn)).astype(BF16), st.astype(BF16), _NT,
                              preferred_element_type=F32)
    k_dec = (kn * jnp.exp(edge - pn)).astype(BF16)
    st_ref[...] = jnp.exp(edge) * st + lax.dot_general(
        vn.astype(BF16), k_dec, (((0,), (0,)), ((), ())), preferred_element_type=F32)
    high = [(q_tiles, jnp.dot(sc, vc, preferred_element_type=F32)) for q_tiles, sc, vc in high]
    yield None
    on = tmp_ref[3]
    ot = [tile(on, i) for i in range(nb8)]
    for q_tiles, ol in high:
        for n, i in enumerate(q_tiles):
            ot[i] = ot[i] + ol[n * sub:(n + 1) * sub]
    yield jnp.concatenate(ot, axis=0) + o_inter


def _hgrn_scan_kernel(layer, *refs):
    hb = SCAN_HB
    in_refs = refs[:6 * hb]
    lb_ref, masks_ref, of_ref, ob_ref, st_ref = refs[6 * hb:6 * hb + 5]
    tmp_refs = refs[6 * hb + 5:]

    @pl.when(pl.program_id(2) == 0)
    def _():
        st_ref[...] = jnp.zeros_like(st_ref)

    z = lb_ref[...]
    ez = jnp.exp(z - jnp.max(z, axis=1, keepdims=True))
    pz = ez / jnp.sum(ez, axis=1, keepdims=True)
    lb = jnp.sum(pz[:, :layer + 1, :], axis=1) - pz[:, 0, :]

    streams = []
    for step in range(SCAN_CH):
        for d, o_ref in enumerate((of_ref, ob_ref)):
            ci = step if d == 0 else SCAN_CH - 1 - step
            for j in range(hb):
                sl = slice(j * LANES, (j + 1) * LANES)
                q_ref, f_ref, v_ref = in_refs[3 * (d * hb + j):3 * (d * hb + j) + 3]
                gen = _scan_chunk(q_ref, f_ref, v_ref, ci * SCAN_C, lb[d:d + 1, sl], st_ref.at[d, j],
                                  tmp_refs[len(streams)], masks_ref, d == 1)
                streams.append((o_ref, ci * SCAN_C, sl, gen))
    outs = [None] * len(streams)
    for _ in range(SCAN_PHASES):
        for n, stream in enumerate(streams):
            outs[n] = next(stream[3])
    for (o_ref, row0, sl, _), out in zip(streams, outs):
        o_ref[0, row0:row0 + SCAN_C, sl] = out


def _hgrn_scan(h, hgrn_lb, layer):
    bsz, seq, _ = h.shape
    c, hb = SCAN_C * SCAN_CH, SCAN_HB
    nc = seq // c

    def col(part, j, rev):
        if rev:
            return pl.BlockSpec((1, c, LANES),
                                lambda i, hg, n: (i, nc - 1 - n, part * N_HEADS + hg * hb + j))
        return pl.BlockSpec((1, c, LANES), lambda i, hg, n: (i, n, part * N_HEADS + hg * hb + j))

    in_specs = []
    for rev, f_part in ((False, 1), (True, 2)):
        for j in range(hb):
            in_specs += [col(0, j, rev), col(f_part, j, rev), col(3, j, rev)]
    in_specs += [pl.BlockSpec((2, DEPTH, hb * LANES), lambda i, hg, n: (0, 0, hg)),
                 _const_spec((SCAN_LEVELS, SCAN_C // 2, SCAN_C // 2))]
    out_f = pl.BlockSpec((1, c, hb * LANES), lambda i, hg, n: (i, n, hg))
    out_b = pl.BlockSpec((1, c, hb * LANES), lambda i, hg, n: (i, nc - 1 - n, hg))
    return pl.pallas_call(
        functools.partial(_hgrn_scan_kernel, layer),
        grid=(bsz, N_HEADS // hb, nc),
        in_specs=in_specs,
        out_specs=[out_f, out_b],
        out_shape=[jax.ShapeDtypeStruct((bsz, seq, D_INNER), F32)] * 2,
        scratch_shapes=[pltpu.VMEM((2, hb, HEAD_DIM, HEAD_DIM), F32)]
        + [pltpu.VMEM((4, SCAN_C, LANES), F32)] * (2 * hb * SCAN_CH),
        compiler_params=pltpu.CompilerParams(
            dimension_semantics=("arbitrary", "arbitrary", "arbitrary"), vmem_limit_bytes=VMEM_LIMIT),
        name="hgrn_scan",
    )(*([h] * (6 * hb)), hgrn_lb, jnp.asarray(_scan_masks()))


def _hgrn_out_kernel(of_ref, ob_ref, g_ref, x_ref, ng_ref, wout_ref, g2_ref, b2_ref, o_ref, v_ref):
    for hd in range(N_HEADS):
        sl = slice(hd * HEAD_DIM, (hd + 1) * HEAD_DIM)
        o = of_ref[:, sl] + ob_ref[:, sl]
        o = o * lax.rsqrt(jnp.mean(o * o, axis=-1, keepdims=True) + LN_EPS)
        v_ref[:, sl] = (o * ng_ref[:, sl] * _silu(g_ref[:, sl])).astype(BF16)
    y = jnp.dot(v_ref[...], wout_ref[...], preferred_element_type=F32)
    o_ref[...] = _res_ln(x_ref[...], y, g2_ref[...], b2_ref[...])


def _hgrn_out(o_fw, o_bw, h2d, x2d, norm_g, w_out, g2, b2):
    t, d = x2d.shape
    tm = OUT_TM
    row = lambda i: (i, 0)
    return pl.pallas_call(
        _hgrn_out_kernel,
        grid=(t // tm,),
        in_specs=[pl.BlockSpec((tm, D_INNER), row), pl.BlockSpec((tm, D_INNER), row),
                  pl.BlockSpec((tm, D_INNER), lambda i: (i, 4)),
                  pl.BlockSpec((tm, d), row),
                  _const_spec((1, D_INNER)), _const_spec((D_INNER, d)),
                  _const_spec((1, d)), _const_spec((1, d))],
        out_specs=pl.BlockSpec((tm, d), row),
        out_shape=jax.ShapeDtypeStruct((t, d), F32),
        scratch_shapes=[pltpu.VMEM((tm, D_INNER), BF16)],
        compiler_params=pltpu.CompilerParams(dimension_semantics=("arbitrary",),
                                             vmem_limit_bytes=VMEM_LIMIT),
        name="hgrn_out",
    )(o_fw, o_bw, h2d, x2d, norm_g.reshape(1, -1), w_out.astype(BF16), g2.reshape(1, -1),
      b2.reshape(1, -1))


def _hgrn_layer(x, w_in, hgrn_lb, layer, norm_g, w_out, g2, b2):
    bsz, seq, d = x.shape
    x2d = x.reshape(bsz * seq, d)
    h2d = _in_proj(x2d, w_in.astype(BF16), F32)
    o_fw, o_bw = _hgrn_scan(h2d.reshape(bsz, seq, -1), hgrn_lb, layer)
    y = _hgrn_out(o_fw.reshape(bsz * seq, -1), o_bw.reshape(bsz * seq, -1), h2d, x2d,
                  norm_g, w_out, g2, b2)
    return y.reshape(bsz, seq, d)


def _t5_bucket(rel):
    nb = REL_BUCKETS // 2
    max_exact = nb // 2
    ret = (rel > 0).astype(jnp.int32) * nb
    n = jnp.abs(rel)
    large = max_exact + (jnp.log(jnp.maximum(n, 1).astype(jnp.float32) / max_exact)
                         / math.log(REL_MAX_DIST / max_exact) * (nb - max_exact)).astype(jnp.int32)
    large = jnp.minimum(large, nb - 1)
    return ret + jnp.where(n < max_exact, n, large)


def _attn_layer_kernel(seq, q_ref, kp_ref, kc_ref, kn_ref, vp_ref, vc_ref, vn_ref, g_ref, x_ref,
                       bucket_ref, relb_ref, sink_ref, wout_ref, g2_ref, b2_ref, o_ref,
                       bias_ref, v_ref):
    n = pl.program_id(1)

    @pl.when((pl.program_id(0) == 0) & (n == 0))
    def _():
        bucket = bucket_ref[...]
        for h in range(N_HEADS):
            acc = jnp.zeros(bucket.shape, F32)
            for b in range(REL_BUCKETS):
                acc = jnp.where(bucket == b, relb_ref[b, h], acc)
            bias_ref[h] = acc

    kwin = jnp.concatenate([kp_ref[0], kc_ref[0], kn_ref[0]], axis=0)
    vwin = jnp.concatenate([vp_ref[0], vc_ref[0], vn_ref[0]], axis=0)
    qb = q_ref[0]

    rows = GROUP * BLOCK
    col = lax.broadcasted_iota(jnp.int32, (rows, 3 * BLOCK), 1)
    qpos = lax.broadcasted_iota(jnp.int32, (rows, 3 * BLOCK), 0) & (BLOCK - 1)
    rel = col - BLOCK - qpos
    key_abs = n * BLOCK + col - BLOCK
    valid = (jnp.abs(rel) <= WINDOW) & (key_abs >= 0) & (key_abs < seq)
    head_of_row = lax.broadcasted_iota(jnp.int32, (rows, 1), 0) // BLOCK

    for kh in range(KV_HEADS):
        ksl = slice(kh * HEAD_DIM, (kh + 1) * HEAD_DIM)
        h0 = kh * GROUP
        qg = jnp.concatenate([qb[:, (h0 + gi) * HEAD_DIM:(h0 + gi + 1) * HEAD_DIM]
                              for gi in range(GROUP)], axis=0)
        s = lax.dot_general(qg, kwin[:, ksl], (((1,), (1,)), ((), ())),
                            preferred_element_type=F32)
        bias = bias_ref[pl.ds(h0, GROUP)].reshape(rows, 3 * BLOCK)
        s = jnp.where(valid, s * ATTN_SCALE + bias, -jnp.inf)
        sink = jnp.zeros((rows, 1), F32)
        for gi in range(GROUP):
            sink = jnp.where(head_of_row == gi, sink_ref[h0 + gi], sink)
        mx = jnp.maximum(jnp.max(s, axis=-1, keepdims=True), sink)
        p = jnp.exp(s - mx)
        den = jnp.sum(p, axis=-1, keepdims=True) + jnp.exp(sink - mx)
        o = jnp.dot(p.astype(BF16), vwin[:, ksl], preferred_element_type=F32) / den
        for gi in range(GROUP):
            sl = slice((h0 + gi) * HEAD_DIM, (h0 + gi + 1) * HEAD_DIM)
            v_ref[:, sl] = (o[gi * BLOCK:(gi + 1) * BLOCK] * _silu(g_ref[0, :, sl])).astype(BF16)

    y = jnp.dot(v_ref[...], wout_ref[...], preferred_element_type=F32)
    o_ref[0] = _res_ln(x_ref[0], y, g2_ref[...], b2_ref[...])


def _attn_layer(x, w_in, sink, rel_bias, w_out, g2, b2):
    bsz, seq, d = x.shape
    nb = seq // BLOCK
    w_bf = w_in.astype(BF16)
    x2d = x.reshape(bsz * seq, d)
    qkv = _in_proj(x2d, w_bf[:, :D_INNER + 2 * KV_WIDTH], BF16).reshape(bsz, seq, -1)
    gate = _in_proj(x2d, w_bf[:, D_INNER + 2 * KV_WIDTH:], F32).reshape(bsz, seq, -1)
    q_pos = jnp.arange(BLOCK)
    k_off = jnp.arange(3 * BLOCK) - BLOCK
    bucket = _t5_bucket(k_off[None, :] - q_pos[:, None])
    k_col = D_INNER // KV_WIDTH
    v_col = k_col + 1
    prev = lambda i, n: jnp.maximum(n - 1, 0)
    nxt = lambda i, n: jnp.minimum(n + 1, nb - 1)
    kv = lambda colb, f: pl.BlockSpec((1, BLOCK, KV_WIDTH), lambda i, n: (i, f(i, n), colb))
    cur = lambda i, n: n
    return pl.pallas_call(
        functools.partial(_attn_layer_kernel, seq),
        grid=(bsz, nb),
        in_specs=[
            pl.BlockSpec((1, BLOCK, D_INNER), lambda i, n: (i, n, 0)),
            kv(k_col, prev), kv(k_col, cur), kv(k_col, nxt),
            kv(v_col, prev), kv(v_col, cur), kv(v_col, nxt),
            pl.BlockSpec((1, BLOCK, D_INNER), lambda i, n: (i, n, 0)),
            pl.BlockSpec((1, BLOCK, d), lambda i, n: (i, n, 0)),
            _const_spec((BLOCK, 3 * BLOCK)),
            pl.BlockSpec(memory_space=pltpu.SMEM),
            pl.BlockSpec(memory_space=pltpu.SMEM),
            _const_spec((D_INNER, d)),
            _const_spec((1, d)),
            _const_spec((1, d)),
        ],
        out_specs=pl.BlockSpec((1, BLOCK, d), lambda i, n: (i, n, 0)),
        out_shape=jax.ShapeDtypeStruct((bsz, seq, d), F32),
        scratch_shapes=[pltpu.VMEM((N_HEADS, BLOCK, 3 * BLOCK), F32),
                        pltpu.VMEM((BLOCK, D_INNER), BF16)],
        compiler_params=pltpu.CompilerParams(dimension_semantics=("arbitrary", "arbitrary"),
                                             vmem_limit_bytes=VMEM_LIMIT),
        name="attn_layer",
    )(qkv, qkv, qkv, qkv, qkv, qkv, qkv, gate, x, bucket, rel_bias, sink, w_out.astype(BF16),
      g2.reshape(1, -1), b2.reshape(1, -1))


def _trunk(x, ln_g, ln_b, w_in_conv, conv_w, conv_b, conv_ln_g, conv_ln_b, w_out_conv,
           w_in_hgrn, hgrn_lb, hgrn_norm_g, w_out_hgrn, w_in_attn, attn_sink, rel_bias, w_out_attn):
    for i in range(DEPTH):
        j = i // N_MIXERS
        kind = i % N_MIXERS
        if kind == 0:
            x = _conv_layer(x, w_in_conv[j], conv_w[j], conv_b[j], conv_ln_g[j], conv_ln_b[j],
                            w_out_conv[j], ln_g[i], ln_b[i])
        elif kind == 1:
            x = _hgrn_layer(x, w_in_hgrn[j], hgrn_lb, i, hgrn_norm_g[j], w_out_hgrn[j],
                            ln_g[i], ln_b[i])
        else:
            x = _attn_layer(x, w_in_attn[j], attn_sink[j], rel_bias, w_out_attn[j], ln_g[i], ln_b[i])
    return x


def kernel(x_prompt, x_sample, ln_g, ln_b, w_in_conv, conv_w, conv_b, conv_ln_g, conv_ln_b,
           w_out_conv, w_in_hgrn, hgrn_lb, hgrn_norm_g, w_out_hgrn, w_in_attn, attn_sink,
           rel_bias, w_out_attn):
    params = (ln_g, ln_b, w_in_conv, conv_w, conv_b, conv_ln_g, conv_ln_b, w_out_conv,
              w_in_hgrn, hgrn_lb, hgrn_norm_g, w_out_hgrn, w_in_attn, attn_sink, rel_bias, w_out_attn)
    return (_trunk(x_prompt, *params), _trunk(x_sample, *params))
```

```python
import functools
import math

import jax
import jax.numpy as jnp
import numpy as np
from jax import lax
from jax.experimental import pallas as pl
from jax.experimental.pallas import tpu as pltpu

D_MODEL = 1024
DEPTH = 4
N_MIXERS = 3
D_INNER = 2 * D_MODEL
CONV_WIDTH = 31
CONV_HALF = CONV_WIDTH // 2
HEAD_DIM = 128
N_HEADS = D_INNER // HEAD_DIM
KV_HEADS = 4
GROUP = N_HEADS // KV_HEADS
KV_WIDTH = KV_HEADS * HEAD_DIM
ATTN_SCALE = HEAD_DIM ** -0.5
WINDOW = 128
BLOCK = 128
REL_BUCKETS = 32
REL_MAX_DIST = 128
ALPHA = (2 * DEPTH) ** 0.25
LN_EPS = 1e-5

LANES = 128
SUBLANES = 8
VMEM_LIMIT = 56 * 1024 * 1024

CONV_TL = 256
CONV_HALO = 16
CONV_CB = 512
CONV_RC = 128
SCAN_C = 128
SCAN_LEVELS = SCAN_C.bit_length() - 1
SCAN_HB = 2
SCAN_CH = 2
SCAN_PHASES = 6
PROJ_TM = 1024
PROJ_TN = 1024
OUT_TM = 256
F_FLOOR = 1e-37

F32 = jnp.float32
BF16 = jnp.bfloat16


def _const_spec(shape):
    return pl.BlockSpec(shape, lambda *_: (0,) * len(shape), pipeline_mode=pl.Buffered(1))


def _sigmoid(x):
    return 1.0 / (1.0 + jnp.exp(-x))


def _silu(x):
    return x * _sigmoid(x)


def _res_ln(x, y, g, b):
    z = ALPHA * x + y
    mu = jnp.mean(z, axis=-1, keepdims=True)
    d = z - mu
    var = jnp.mean(d * d, axis=-1, keepdims=True)
    return d * lax.rsqrt(var + LN_EPS) * g + b


def _proj_kernel(x_ref, w_ref, o_ref):
    o_ref[...] = jnp.dot(x_ref[...].astype(BF16), w_ref[...],
                         preferred_element_type=F32).astype(o_ref.dtype)


def _in_proj(x2d, w_bf, out_dtype):
    t, d = x2d.shape
    n = w_bf.shape[1]
    return pl.pallas_call(
        _proj_kernel,
        grid=(t // PROJ_TM, n // PROJ_TN),
        in_specs=[pl.BlockSpec((PROJ_TM, d), lambda i, j: (i, 0)),
                  pl.BlockSpec((d, PROJ_TN), lambda i, j: (0, j))],
        out_specs=pl.BlockSpec((PROJ_TM, PROJ_TN), lambda i, j: (i, j)),
        out_shape=jax.ShapeDtypeStruct((t, n), out_dtype),
        compiler_params=pltpu.CompilerParams(dimension_semantics=("arbitrary", "arbitrary"),
                                             vmem_limit_bytes=VMEM_LIMIT),
        name="in_proj",
    )(x2d, w_bf)


def _proj_pair_kernel(x_ref, w_ref, o_ref):
    acc = jnp.dot(x_ref[...].astype(BF16), w_ref[...], preferred_element_type=F32)
    half = acc.shape[1] // 2
    lo = lax.bitcast_convert_type(acc[:, :half].astype(BF16).astype(F32), jnp.uint32) >> 16
    hi = lax.bitcast_convert_type(acc[:, half:].astype(BF16).astype(F32), jnp.uint32)
    o_ref[...] = lo | hi


def _in_proj_pairs(x2d, w_bf):
    t, d = x2d.shape
    n = w_bf.shape[1]
    return pl.pallas_call(
        _proj_pair_kernel,
        grid=(t // PROJ_TM, n // PROJ_TN),
        in_specs=[pl.BlockSpec((PROJ_TM, d), lambda i, j: (i, 0)),
                  pl.BlockSpec((d, PROJ_TN), lambda i, j: (0, j))],
        out_specs=pl.BlockSpec((PROJ_TM, PROJ_TN // 2), lambda i, j: (i, j)),
        out_shape=jax.ShapeDtypeStruct((t, n // 2), jnp.uint32),
        compiler_params=pltpu.CompilerParams(dimension_semantics=("arbitrary", "arbitrary"),
                                             vmem_limit_bytes=VMEM_LIMIT),
        name="in_proj_pairs",
    )(x2d, w_bf)


def _conv_layer_kernel(xp_ref, xc_ref, xn_ref, wab_ref, wg_ref, cw_ref, cb_ref, lng_ref, lnb_ref,
                       wout_ref, g2_ref, b2_ref, o_ref, uext_ref, conv_ref, sg_ref, v_ref):
    tl, halo = CONV_TL, CONV_HALO
    n_lane_blocks = D_INNER // LANES
    t = pl.program_id(1)
    nt = pl.num_programs(1)
    xc = xc_ref[0]
    xe = jnp.concatenate([xp_ref[0], xc, xn_ref[0]], axis=0).astype(BF16)
    xcb = xe[halo:halo + tl]

    rows = lax.broadcasted_iota(jnp.int32, (tl + 2 * halo, CONV_CB), 0)
    lo = jnp.where(t > 0, 0, halo)
    hi = jnp.where(t < nt - 1, tl + 2 * halo, tl + halo)
    inside = (rows >= lo) & (rows < hi)
    per_cb = CONV_CB // LANES
    for cb in range(D_INNER // CONV_CB):
        ab = jnp.dot(xe, wab_ref[cb], preferred_element_type=F32)
        u = ab[:, :CONV_CB] * _sigmoid(ab[:, CONV_CB:])
        u = jnp.where(inside, u, 0.0)
        for j in range(per_cb):
            uext_ref[cb * per_cb + j] = u[:, j * LANES:(j + 1) * LANES]
        g = jnp.dot(xcb, wg_ref[cb], preferred_element_type=F32)
        sg_ref[:, cb * CONV_CB:(cb + 1) * CONV_CB] = _silu(g)

    def lane_block(j, carry):
        w = cw_ref[j]
        bias = cb_ref[j]
        for rc in range(tl // CONV_RC):
            acc = jnp.broadcast_to(bias, (CONV_RC, LANES))
            for k in range(CONV_WIDTH):
                start = rc * CONV_RC + k + halo - CONV_HALF
                acc = acc + w[k:k + 1, :] * uext_ref[j, pl.ds(start, CONV_RC), :]
            conv_ref[j, pl.ds(rc * CONV_RC, CONV_RC), :] = acc
        return carry

    lax.fori_loop(0, n_lane_blocks, lane_block, 0)

    s1 = conv_ref[0]
    for j in range(1, n_lane_blocks):
        s1 = s1 + conv_ref[j]
    mu = jnp.sum(s1, axis=-1, keepdims=True) * (1.0 / D_INNER)
    s2 = jnp.zeros((tl, LANES), F32)
    for j in range(n_lane_blocks):
        d = conv_ref[j] - mu
        s2 = s2 + d * d
    rstd = lax.rsqrt(jnp.sum(s2, axis=-1, keepdims=True) * (1.0 / D_INNER) + LN_EPS)
    for j in range(n_lane_blocks):
        sl = slice(j * LANES, (j + 1) * LANES)
        c = (conv_ref[j] - mu) * rstd * lng_ref[:, sl] + lnb_ref[:, sl]
        v_ref[:, sl] = (_silu(c) * sg_ref[:, sl]).astype(BF16)

    y = jnp.dot(v_ref[...], wout_ref[...], preferred_element_type=F32)
    o_ref[0] = _res_ln(xc, y, g2_ref[...], b2_ref[...])


def _conv_layer(x, w_in, conv_w, conv_b, ln_g, ln_b, w_out, g2, b2):
    bsz, seq, d = x.shape
    tl, halo = CONV_TL, CONV_HALO
    n_cb = D_INNER // CONV_CB
    n_lb = D_INNER // LANES
    a, b, g = jnp.split(w_in.astype(BF16), 3, axis=-1)
    wab = jnp.concatenate([a.reshape(d, n_cb, CONV_CB), b.reshape(d, n_cb, CONV_CB)], axis=-1)
    wab = wab.transpose(1, 0, 2)
    wg = g.reshape(d, n_cb, CONV_CB).transpose(1, 0, 2)
    cw = jnp.pad(conv_w, ((0, 1), (0, 0))).reshape(CONV_WIDTH + 1, n_lb, LANES).transpose(1, 0, 2)
    cb = conv_b.reshape(n_lb, 1, LANES)
    hpt = tl // halo
    last_halo = seq // halo - 1
    return pl.pallas_call(
        _conv_layer_kernel,
        grid=(bsz, seq // tl),
        in_specs=[
            pl.BlockSpec((1, halo, d), lambda i, t: (i, jnp.maximum(t * hpt - 1, 0), 0)),
            pl.BlockSpec((1, tl, d), lambda i, t: (i, t, 0)),
            pl.BlockSpec((1, halo, d), lambda i, t: (i, jnp.minimum((t + 1) * hpt, last_halo), 0)),
            _const_spec((n_cb, d, 2 * CONV_CB)),
            _const_spec((n_cb, d, CONV_CB)),
            _const_spec((n_lb, CONV_WIDTH + 1, LANES)),
            _const_spec((n_lb, 1, LANES)),
            _const_spec((1, D_INNER)),
            _const_spec((1, D_INNER)),
            _const_spec((D_INNER, d)),
            _const_spec((1, d)),
            _const_spec((1, d)),
        ],
        out_specs=pl.BlockSpec((1, tl, d), lambda i, t: (i, t, 0)),
        out_shape=jax.ShapeDtypeStruct((bsz, seq, d), F32),
        scratch_shapes=[
            pltpu.VMEM((n_lb, tl + 2 * halo, LANES), F32),
            pltpu.VMEM((n_lb, tl, LANES), F32),
            pltpu.VMEM((tl, D_INNER), F32),
            pltpu.VMEM((tl, D_INNER), BF16),
        ],
        compiler_params=pltpu.CompilerParams(dimension_semantics=("arbitrary", "arbitrary"),
                                             vmem_limit_bytes=VMEM_LIMIT),
        name="conv_layer",
    )(x, x, x, wab, wg, cw, cb, ln_g.reshape(1, -1), ln_b.reshape(1, -1), w_out.astype(BF16),
      g2.reshape(1, -1), b2.reshape(1, -1))


def _scan_masks():
    half, nb8 = SCAN_C // 2, SCAN_C // SUBLANES
    i = np.arange(half)[:, None]
    j = np.arange(half)[None, :]
    same8 = (i % nb8) == (j % nb8)
    masks = [same8 & (i // (nb8 << lvl) == j // (nb8 << lvl)) for lvl in range(3)]
    masks += [i // (SUBLANES << lvl) == j // (SUBLANES << lvl) for lvl in range(SCAN_LEVELS - 3)]
    return np.stack(masks).astype(np.float32)


def _gates(raw, lb):
    e_abs = jnp.exp(-jnp.abs(raw))
    r_abs = 1.0 / (1.0 + e_abs)
    er = e_abs * r_abs
    pos = raw >= 0
    c1 = 1.0 - lb
    k = c1 * jnp.where(pos, er, r_abs)
    f = lb + c1 * jnp.where(pos, r_abs, er)
    return k, jnp.log(jnp.maximum(f, F_FLOOR))


_NT = (((1,), (1,)), ((), ()))


def _level_scores(qc, kc, mask):
    s = lax.dot_general(qc.astype(BF16), kc.astype(BF16), _NT, preferred_element_type=F32)
    if mask is not None:
        s = s * mask
    return s.astype(BF16)


def _scan_chunk(q_ref, f_ref, v_ref, half, row0, lb, st_ref, tmp_ref, masks_ref, reverse):
    c, sub = SCAN_C, SUBLANES
    nb8 = c // sub

    def unpack(words):
        bits = (words << 16) if half == 0 else (words & jnp.uint32(0xFFFF0000))
        return lax.bitcast_convert_type(bits, F32)

    def piece(ref, r):
        return unpack(ref[0, pl.ds(row0 + r, nb8, stride=sub), :])

    q = [_silu(piece(q_ref, r)) for r in range(sub)]
    kp = [_gates(piece(f_ref, r), lb) for r in range(sub)]
    k = [a for a, _ in kp]
    p = [b for _, b in kp]
    v = [piece(v_ref, r) for r in range(sub)]
    o = [jnp.sum(q[r] * k[r], axis=-1, keepdims=True) * v[r] for r in range(sub)]
    yield None
    low = []
    for lvl in range(3):
        m = 1 << lvl
        q_rows, k_rows, e_of = [], [], {}
        for blk in range(0, sub, 2 * m):
            lo, hi = list(range(blk, blk + m)), list(range(blk + m, blk + 2 * m))
            q_rows += lo if reverse else hi
            k_rows += hi if reverse else lo
            for r in lo + hi:
                e_of[r] = blk + m if reverse else blk + m - 1
        qc = jnp.concatenate([q[r] * jnp.exp(p[r]) for r in q_rows], axis=0)
        kc = jnp.concatenate([k[r] if r == e_of[r] else k[r] * jnp.exp(p[e_of[r]] - p[r])
                              for r in k_rows], axis=0)
        vc = jnp.concatenate([v[r] for r in k_rows], axis=0)
        low.append((q_rows, _level_scores(qc, kc, masks_ref[lvl]), vc.astype(BF16)))
        for r in q_rows:
            p[r] = p[r] + p[e_of[r]]
    for r in range(sub):
        for slot, arr in enumerate((p, q, k)):
            tmp_ref[slot, pl.ds(r, nb8, stride=sub), :] = arr[r]
    yield None
    low = [(q_rows, jnp.dot(sc, vc, preferred_element_type=F32)) for q_rows, sc, vc in low]
    yield None
    for q_rows, ol in low:
        for i, r in enumerate(q_rows):
            o[r] = o[r] + ol[i * nb8:(i + 1) * nb8]
    for r in range(sub):
        tmp_ref[3, pl.ds(r, nb8, stride=sub), :] = o[r]
    tile = lambda x, i: x[i * sub:(i + 1) * sub]
    pn, qn, kn, vn = tmp_ref[0], tmp_ref[1], tmp_ref[2], unpack(v_ref[0, row0:row0 + c, :])
    pt = [tile(pn, i) for i in range(nb8)]
    qt = [tile(qn, i) for i in range(nb8)]
    kt = [tile(kn, i) for i in range(nb8)]
    vt = [tile(vn, i) for i in range(nb8)]
    high = []
    for lvl in range(3, SCAN_LEVELS):
        mt = (1 << lvl) // sub
        q_tiles, k_tiles, eb = [], [], {}
        for blk in range(0, nb8, 2 * mt):
            lo, hi = list(range(blk, blk + mt)), list(range(blk + mt, blk + 2 * mt))
            q_tiles += lo if reverse else hi
            k_tiles += hi if reverse else lo
            e = pt[blk + mt][0:1, :] if reverse else pt[blk + mt - 1][sub - 1:sub, :]
            e = jnp.broadcast_to(e, (sub, LANES))
            for i in lo + hi:
                eb[i] = e
        qc = jnp.concatenate([qt[i] * jnp.exp(pt[i]) for i in q_tiles], axis=0)
        kc = jnp.concatenate([kt[i] * jnp.exp(eb[i] - pt[i]) for i in k_tiles], axis=0)
        vc = jnp.concatenate([vt[i] for i in k_tiles], axis=0)
        high.append((q_tiles, _level_scores(qc, kc, masks_ref[lvl] if lvl < SCAN_LEVELS - 1 else None),
                     vc.astype(BF16)))
        for i in q_tiles:
            pt[i] = pt[i] + eb[i]
    yield None
    pn = jnp.concatenate(pt, axis=0)
    edge = pt[0][0:1, :] if reverse else pt[nb8 - 1][sub - 1:sub, :]
    st = st_ref[...]
    o_inter = lax.dot_general((qn * jnp.exp(pn)).astype(BF16), st.astype(BF16), _NT,
                              preferred_element_type=F32)
    k_dec = (kn * jnp.exp(edge - pn)).astype(BF16)
    st_ref[...] = jnp.exp(edge) * st + lax.dot_general(
        vn.astype(BF16), k_dec, (((0,), (0,)), ((), ())), preferred_element_type=F32)
    high = [(q_tiles, jnp.dot(sc, vc, preferred_element_type=F32)) for q_tiles, sc, vc in high]
    yield None
    on = tmp_ref[3]
    ot = [tile(on, i) for i in range(nb8)]
    for q_tiles, ol in high:
        for n, i in enumerate(q_tiles):
            ot[i] = ot[i] + ol[n * sub:(n + 1) * sub]
    yield jnp.concatenate(ot, axis=0) + o_inter


def _hgrn_scan_kernel(layer, *refs):
    hb = SCAN_HB
    in_refs = refs[:6]
    lb_ref, masks_ref, of_ref, ob_ref, st_ref = refs[6:11]
    tmp_refs = refs[11:]

    @pl.when(pl.program_id(2) == 0)
    def _():
        st_ref[...] = jnp.zeros_like(st_ref)

    z = lb_ref[...]
    ez = jnp.exp(z - jnp.max(z, axis=1, keepdims=True))
    pz = ez / jnp.sum(ez, axis=1, keepdims=True)
    lb = jnp.sum(pz[:, :layer + 1, :], axis=1) - pz[:, 0, :]

    streams = []
    for step in range(SCAN_CH):
        for d, o_ref in enumerate((of_ref, ob_ref)):
            ci = step if d == 0 else SCAN_CH - 1 - step
            for j in range(hb):
                sl = slice(j * LANES, (j + 1) * LANES)
                q_ref, f_ref, v_ref = in_refs[3 * d:3 * d + 3]
                gen = _scan_chunk(q_ref, f_ref, v_ref, j, ci * SCAN_C, lb[d:d + 1, sl], st_ref.at[d, j],
                                  tmp_refs[len(streams)], masks_ref, d == 1)
                streams.append((o_ref, ci * SCAN_C, sl, gen))
    outs = [None] * len(streams)
    for _ in range(SCAN_PHASES):
        for n, stream in enumerate(streams):
            outs[n] = next(stream[3])
    for (o_ref, row0, sl, _), out in zip(streams, outs):
        o_ref[0, row0:row0 + SCAN_C, sl] = out


def _hgrn_scan(h, hgrn_lb, layer):
    bsz, seq, _ = h.shape
    c, hb = SCAN_C * SCAN_CH, SCAN_HB
    assert hb == 2
    nc = seq // c
    n_pairs = N_HEADS // hb

    def col(part, rev):
        if rev:
            return pl.BlockSpec((1, c, LANES), lambda i, hg, n: (i, nc - 1 - n, part * n_pairs + hg))
        return pl.BlockSpec((1, c, LANES), lambda i, hg, n: (i, n, part * n_pairs + hg))

    in_specs = [col(0, False), col(1, False), col(3, False), col(0, True), col(2, True), col(3, True)]
    in_specs += [pl.BlockSpec((2, DEPTH, hb * LANES), lambda i, hg, n: (0, 0, hg)),
                 _const_spec((SCAN_LEVELS, SCAN_C // 2, SCAN_C // 2))]
    out_f = pl.BlockSpec((1, c, hb * LANES), lambda i, hg, n: (i, n, hg))
    out_b = pl.BlockSpec((1, c, hb * LANES), lambda i, hg, n: (i, nc - 1 - n, hg))
    return pl.pallas_call(
        functools.partial(_hgrn_scan_kernel, layer),
        grid=(bsz, N_HEADS // hb, nc),
        in_specs=in_specs,
        out_specs=[out_f, out_b],
        out_shape=[jax.ShapeDtypeStruct((bsz, seq, D_INNER), F32)] * 2,
        scratch_shapes=[pltpu.VMEM((2, hb, HEAD_DIM, HEAD_DIM), F32)]
        + [pltpu.VMEM((4, SCAN_C, LANES), F32)] * (2 * hb * SCAN_CH),
        compiler_params=pltpu.CompilerParams(
            dimension_semantics=("arbitrary", "arbitrary", "arbitrary"), vmem_limit_bytes=VMEM_LIMIT),
        name="hgrn_scan",
    )(h, h, h, h, h, h, hgrn_lb, jnp.asarray(_scan_masks()))


def _hgrn_out_kernel(of_ref, ob_ref, x_ref, wg_ref, ng_ref, wout_ref, g2_ref, b2_ref, o_ref, v_ref):
    x = x_ref[...]
    gate = _silu(jnp.dot(x.astype(BF16), wg_ref[...], preferred_element_type=F32))
    for hd in range(N_HEADS):
        sl = slice(hd * HEAD_DIM, (hd + 1) * HEAD_DIM)
        o = of_ref[:, sl] + ob_ref[:, sl]
        o = o * lax.rsqrt(jnp.mean(o * o, axis=-1, keepdims=True) + LN_EPS)
        v_ref[:, sl] = (o * ng_ref[:, sl] * gate[:, sl]).astype(BF16)
    y = jnp.dot(v_ref[...], wout_ref[...], preferred_element_type=F32)
    o_ref[...] = _res_ln(x, y, g2_ref[...], b2_ref[...])


def _hgrn_out(o_fw, o_bw, x2d, w_gate, norm_g, w_out, g2, b2):
    t, d = x2d.shape
    tm = OUT_TM
    row = lambda i: (i, 0)
    return pl.pallas_call(
        _hgrn_out_kernel,
        grid=(t // tm,),
        in_specs=[pl.BlockSpec((tm, D_INNER), row), pl.BlockSpec((tm, D_INNER), row),
                  pl.BlockSpec((tm, d), row), _const_spec((d, D_INNER)),
                  _const_spec((1, D_INNER)), _const_spec((D_INNER, d)),
                  _const_spec((1, d)), _const_spec((1, d))],
        out_specs=pl.BlockSpec((tm, d), row),
        out_shape=jax.ShapeDtypeStruct((t, d), F32),
        scratch_shapes=[pltpu.VMEM((tm, D_INNER), BF16)],
        compiler_params=pltpu.CompilerParams(dimension_semantics=("arbitrary",),
                                             vmem_limit_bytes=VMEM_LIMIT),
        name="hgrn_out",
    )(o_fw, o_bw, x2d, w_gate, norm_g.reshape(1, -1), w_out.astype(BF16), g2.reshape(1, -1),
      b2.reshape(1, -1))


def _hgrn_layer(x, w_in, hgrn_lb, layer, norm_g, w_out, g2, b2):
    bsz, seq, d = x.shape
    x2d = x.reshape(bsz * seq, d)
    w_bf = w_in.astype(BF16)
    per_tile = PROJ_TN // (2 * HEAD_DIM)
    w4 = w_bf[:, :4 * D_INNER].reshape(d, 4, N_HEADS // (2 * per_tile), per_tile, 2, HEAD_DIM)
    w4 = w4.transpose(0, 1, 2, 4, 3, 5).reshape(d, 4 * D_INNER)
    h = _in_proj_pairs(x2d, w4)
    o_fw, o_bw = _hgrn_scan(h.reshape(bsz, seq, -1), hgrn_lb, layer)
    y = _hgrn_out(o_fw.reshape(bsz * seq, -1), o_bw.reshape(bsz * seq, -1), x2d, w_bf[:, 4 * D_INNER:],
                  norm_g, w_out, g2, b2)
    return y.reshape(bsz, seq, d)


def _t5_bucket(rel):
    nb = REL_BUCKETS // 2
    max_exact = nb // 2
    ret = (rel > 0).astype(jnp.int32) * nb
    n = jnp.abs(rel)
    large = max_exact + (jnp.log(jnp.maximum(n, 1).astype(jnp.float32) / max_exact)
                         / math.log(REL_MAX_DIST / max_exact) * (nb - max_exact)).astype(jnp.int32)
    large = jnp.minimum(large, nb - 1)
    return ret + jnp.where(n < max_exact, n, large)


def _attn_layer_kernel(q_ref, kp_ref, kc_ref, kn_ref, vp_ref, vc_ref, vn_ref, x_ref,
                       bucket_ref, relb_ref, sink_ref, wg_ref, wout_ref, g2_ref, b2_ref, o_ref,
                       bias_ref, v_ref):
    n = pl.program_id(1)
    inv_scale = 1.0 / ATTN_SCALE
    neg_inf = -jnp.inf

    @pl.when((pl.program_id(0) == 0) & (n == 0))
    def _():
        bucket = bucket_ref[...]
        col = lax.broadcasted_iota(jnp.int32, bucket.shape, 1)
        row = lax.broadcasted_iota(jnp.int32, bucket.shape, 0)
        in_band = jnp.abs(col - BLOCK - row) <= WINDOW
        for h in range(N_HEADS):
            acc = jnp.zeros(bucket.shape, F32)
            for b in range(REL_BUCKETS):
                acc = jnp.where(bucket == b, relb_ref[b, h], acc)
            bias_ref[h] = jnp.where(in_band, acc * inv_scale, neg_inf)

    x = x_ref[0]
    gate = _silu(jnp.dot(x.astype(BF16), wg_ref[...], preferred_element_type=F32))
    kwin = jnp.concatenate([kp_ref[0], kc_ref[0], kn_ref[0]], axis=0)
    vwin = jnp.concatenate([vp_ref[0], vc_ref[0], vn_ref[0]], axis=0)
    qb = q_ref[0]
    rows = GROUP * BLOCK
    has_prev = n > 0
    has_next = n < pl.num_programs(1) - 1
    head_of_row = lax.broadcasted_iota(jnp.int32, (rows, 1), 0) // BLOCK

    scores = []
    for kh in range(KV_HEADS):
        h0 = kh * GROUP
        qg = jnp.concatenate([qb[:, (h0 + gi) * HEAD_DIM:(h0 + gi + 1) * HEAD_DIM]
                              for gi in range(GROUP)], axis=0)
        scores.append(lax.dot_general(qg, kwin[:, kh * HEAD_DIM:(kh + 1) * HEAD_DIM], _NT,
                                      preferred_element_type=F32))
    c2 = ATTN_SCALE * math.log2(math.e)
    for kh in range(KV_HEADS):
        h0 = kh * GROUP
        s = scores[kh] + bias_ref[pl.ds(h0, GROUP)].reshape(rows, 3 * BLOCK)
        s = jnp.concatenate([jnp.where(has_prev, s[:, :BLOCK], neg_inf), s[:, BLOCK:2 * BLOCK],
                             jnp.where(has_next, s[:, 2 * BLOCK:], neg_inf)], axis=1)
        sink = jnp.zeros((rows, 1), F32)
        for gi in range(GROUP):
            sink = jnp.where(head_of_row == gi, sink_ref[h0 + gi] * inv_scale, sink)
        mx = jnp.maximum(jnp.max(s, axis=-1, keepdims=True), sink)
        p = jnp.exp2((s - mx) * c2)
        den = jnp.sum(p, axis=-1, keepdims=True) + jnp.exp2((sink - mx) * c2)
        o = jnp.dot(p.astype(BF16), vwin[:, kh * HEAD_DIM:(kh + 1) * HEAD_DIM],
                    preferred_element_type=F32) / den
        for gi in range(GROUP):
            sl = slice((h0 + gi) * HEAD_DIM, (h0 + gi + 1) * HEAD_DIM)
            v_ref[:, sl] = (o[gi * BLOCK:(gi + 1) * BLOCK] * gate[:, sl]).astype(BF16)

    y = jnp.dot(v_ref[...], wout_ref[...], preferred_element_type=F32)
    o_ref[0] = _res_ln(x, y, g2_ref[...], b2_ref[...])


def _attn_layer(x, w_in, sink, rel_bias, w_out, g2, b2):
    bsz, seq, d = x.shape
    nb = seq // BLOCK
    w_bf = w_in.astype(BF16)
    qkv = _in_proj(x.reshape(bsz * seq, d), w_bf[:, :D_INNER + 2 * KV_WIDTH], BF16).reshape(bsz, seq, -1)
    q_pos = jnp.arange(BLOCK)
    k_off = jnp.arange(3 * BLOCK) - BLOCK
    bucket = _t5_bucket(k_off[None, :] - q_pos[:, None])
    k_col = D_INNER // KV_WIDTH
    v_col = k_col + 1
    prev = lambda i, n: jnp.maximum(n - 1, 0)
    nxt = lambda i, n: jnp.minimum(n + 1, nb - 1)
    kv = lambda colb, f: pl.BlockSpec((1, BLOCK, KV_WIDTH), lambda i, n: (i, f(i, n), colb))
    cur = lambda i, n: n
    return pl.pallas_call(
        _attn_layer_kernel,
        grid=(bsz, nb),
        in_specs=[
            pl.BlockSpec((1, BLOCK, D_INNER), lambda i, n: (i, n, 0)),
            kv(k_col, prev), kv(k_col, cur), kv(k_col, nxt),
            kv(v_col, prev), kv(v_col, cur), kv(v_col, nxt),
            pl.BlockSpec((1, BLOCK, d), lambda i, n: (i, n, 0)),
            _const_spec((BLOCK, 3 * BLOCK)),
            pl.BlockSpec(memory_space=pltpu.SMEM),
            pl.BlockSpec(memory_space=pltpu.SMEM),
            _const_spec((d, D_INNER)),
            _const_spec((D_INNER, d)),
            _const_spec((1, d)),
            _const_spec((1, d)),
        ],
        out_specs=pl.BlockSpec((1, BLOCK, d), lambda i, n: (i, n, 0)),
        out_shape=jax.ShapeDtypeStruct((bsz, seq, d), F32),
        scratch_shapes=[pltpu.VMEM((N_HEADS, BLOCK, 3 * BLOCK), F32),
                        pltpu.VMEM((BLOCK, D_INNER), BF16)],
        compiler_params=pltpu.CompilerParams(dimension_semantics=("arbitrary", "arbitrary"),
                                             vmem_limit_bytes=VMEM_LIMIT),
        name="attn_layer",
    )(qkv, qkv, qkv, qkv, qkv, qkv, qkv, x, bucket, rel_bias, sink, w_bf[:, D_INNER + 2 * KV_WIDTH:],
      w_out.astype(BF16), g2.reshape(1, -1), b2.reshape(1, -1))


def _trunk(x, ln_g, ln_b, w_in_conv, conv_w, conv_b, conv_ln_g, conv_ln_b, w_out_conv,
           w_in_hgrn, hgrn_lb, hgrn_norm_g, w_out_hgrn, w_in_attn, attn_sink, rel_bias, w_out_attn):
    for i in range(DEPTH):
        j = i // N_MIXERS
        kind = i % N_MIXERS
        if kind == 0:
            x = _conv_layer(x, w_in_conv[j], conv_w[j], conv_b[j], conv_ln_g[j], conv_ln_b[j],
                            w_out_conv[j], ln_g[i], ln_b[i])
        elif kind == 1:
            x = _hgrn_layer(x, w_in_hgrn[j], hgrn_lb, i, hgrn_norm_g[j], w_out_hgrn[j],
                            ln_g[i], ln_b[i])
        else:
            x = _attn_layer(x, w_in_attn[j], attn_sink[j], rel_bias, w_out_attn[j], ln_g[i], ln_b[i])
    return x


def kernel(x_prompt, x_sample, ln_g, ln_b, w_in_conv, conv_w, conv_b, conv_ln_g, conv_ln_b,
           w_out_conv, w_in_hgrn, hgrn_lb, hgrn_norm_g, w_out_hgrn, w_in_attn, attn_sink,
           rel_bias, w_out_attn):
    params = (ln_g, ln_b, w_in_conv, conv_w, conv_b, conv_ln_g, conv_ln_b, w_out_conv,
              w_in_hgrn, hgrn_lb, hgrn_norm_g, w_out_hgrn, w_in_attn, attn_sink, rel_bias, w_out_attn)
    return (_trunk(x_prompt, *params), _trunk(x_sample, *params))
```

```python
import functools
import math

import jax
import jax.numpy as jnp
import numpy as np
from jax import lax
from jax.experimental import pallas as pl
from jax.experimental.pallas import tpu as pltpu

D_MODEL = 1024
DEPTH = 4
N_MIXERS = 3
D_INNER = 2 * D_MODEL
CONV_WIDTH = 31
CONV_HALF = CONV_WIDTH // 2
HEAD_DIM = 128
N_HEADS = D_INNER // HEAD_DIM
KV_HEADS = 4
GROUP = N_HEADS // KV_HEADS
KV_WIDTH = KV_HEADS * HEAD_DIM
ATTN_SCALE = HEAD_DIM ** -0.5
WINDOW = 128
BLOCK = 128
REL_BUCKETS = 32
REL_MAX_DIST = 128
ALPHA = (2 * DEPTH) ** 0.25
LN_EPS = 1e-5

LANES = 128
SUBLANES = 8
VMEM_LIMIT = 56 * 1024 * 1024

CONV_TL = 256
CONV_HALO = 16
CONV_CB = 512
CONV_RC = 128
SCAN_C = 128
SCAN_LEVELS = SCAN_C.bit_length() - 1
SCAN_HB = 2
SCAN_CH = 4
SCAN_PHASES = 6
PROJ_TM = 2048
PROJ_TN = 1024
OUT_TM = 256
F_FLOOR = 1e-37

F32 = jnp.float32
BF16 = jnp.bfloat16


def _const_spec(shape):
    return pl.BlockSpec(shape, lambda *_: (0,) * len(shape), pipeline_mode=pl.Buffered(1))


def _sigmoid(x):
    return 1.0 / (1.0 + jnp.exp(-x))


def _silu(x):
    return x * _sigmoid(x)


def _res_ln(x, y, g, b):
    z = ALPHA * x + y
    mu = jnp.mean(z, axis=-1, keepdims=True)
    d = z - mu
    var = jnp.mean(d * d, axis=-1, keepdims=True)
    return d * lax.rsqrt(var + LN_EPS) * g + b


def _proj_kernel(x_ref, w_ref, o_ref):
    o_ref[...] = jnp.dot(x_ref[...].astype(BF16), w_ref[...],
                         preferred_element_type=F32).astype(o_ref.dtype)


def _in_proj(x2d, w_bf, out_dtype):
    t, d = x2d.shape
    n = w_bf.shape[1]
    return pl.pallas_call(
        _proj_kernel,
        grid=(t // PROJ_TM, n // PROJ_TN),
        in_specs=[pl.BlockSpec((PROJ_TM, d), lambda i, j: (i, 0)),
                  pl.BlockSpec((d, PROJ_TN), lambda i, j: (0, j))],
        out_specs=pl.BlockSpec((PROJ_TM, PROJ_TN), lambda i, j: (i, j)),
        out_shape=jax.ShapeDtypeStruct((t, n), out_dtype),
        compiler_params=pltpu.CompilerParams(dimension_semantics=("arbitrary", "arbitrary"),
                                             vmem_limit_bytes=VMEM_LIMIT),
        name="in_proj",
    )(x2d, w_bf)


def _proj_pair_kernel(x_ref, w_ref, o_ref):
    acc = jnp.dot(x_ref[...].astype(BF16), w_ref[...], preferred_element_type=F32)
    half = acc.shape[1] // 2
    lo = lax.bitcast_convert_type(acc[:, :half].astype(BF16).astype(F32), jnp.uint32) >> 16
    hi = lax.bitcast_convert_type(acc[:, half:].astype(BF16).astype(F32), jnp.uint32)
    o_ref[...] = lo | hi


def _in_proj_pairs(x2d, w_bf):
    t, d = x2d.shape
    n = w_bf.shape[1]
    return pl.pallas_call(
        _proj_pair_kernel,
        grid=(t // PROJ_TM, n // PROJ_TN),
        in_specs=[pl.BlockSpec((PROJ_TM, d), lambda i, j: (i, 0)),
                  pl.BlockSpec((d, PROJ_TN), lambda i, j: (0, j))],
        out_specs=pl.BlockSpec((PROJ_TM, PROJ_TN // 2), lambda i, j: (i, j)),
        out_shape=jax.ShapeDtypeStruct((t, n // 2), jnp.uint32),
        compiler_params=pltpu.CompilerParams(dimension_semantics=("arbitrary", "arbitrary"),
                                             vmem_limit_bytes=VMEM_LIMIT),
        name="in_proj_pairs",
    )(x2d, w_bf)


def _conv_layer_kernel(xp_ref, xc_ref, xn_ref, wab_ref, wg_ref, cw_ref, cb_ref, lng_ref, lnb_ref,
                       wout_ref, g2_ref, b2_ref, o_ref, uext_ref, conv_ref, sg_ref, v_ref):
    tl, halo = CONV_TL, CONV_HALO
    n_lane_blocks = D_INNER // LANES
    t = pl.program_id(1)
    nt = pl.num_programs(1)
    xc = xc_ref[0]
    xe = jnp.concatenate([xp_ref[0], xc, xn_ref[0]], axis=0).astype(BF16)
    xcb = xe[halo:halo + tl]

    rows = lax.broadcasted_iota(jnp.int32, (tl + 2 * halo, CONV_CB), 0)
    lo = jnp.where(t > 0, 0, halo)
    hi = jnp.where(t < nt - 1, tl + 2 * halo, tl + halo)
    inside = (rows >= lo) & (rows < hi)
    per_cb = CONV_CB // LANES
    for cb in range(D_INNER // CONV_CB):
        ab = jnp.dot(xe, wab_ref[cb], preferred_element_type=F32)
        u = ab[:, :CONV_CB] * _sigmoid(ab[:, CONV_CB:])
        u = jnp.where(inside, u, 0.0)
        for j in range(per_cb):
            uext_ref[cb * per_cb + j] = u[:, j * LANES:(j + 1) * LANES]
        g = jnp.dot(xcb, wg_ref[cb], preferred_element_type=F32)
        sg_ref[:, cb * CONV_CB:(cb + 1) * CONV_CB] = _silu(g)

    def lane_block(j, carry):
        w = cw_ref[j]
        bias = cb_ref[j]
        for rc in range(tl // CONV_RC):
            acc = jnp.broadcast_to(bias, (CONV_RC, LANES))
            for k in range(CONV_WIDTH):
                start = rc * CONV_RC + k + halo - CONV_HALF
                acc = acc + w[k:k + 1, :] * uext_ref[j, pl.ds(start, CONV_RC), :]
            conv_ref[j, pl.ds(rc * CONV_RC, CONV_RC), :] = acc
        return carry

    lax.fori_loop(0, n_lane_blocks, lane_block, 0)

    s1 = conv_ref[0]
    for j in range(1, n_lane_blocks):
        s1 = s1 + conv_ref[j]
    mu = jnp.sum(s1, axis=-1, keepdims=True) * (1.0 / D_INNER)
    s2 = jnp.zeros((tl, LANES), F32)
    for j in range(n_lane_blocks):
        d = conv_ref[j] - mu
        s2 = s2 + d * d
    rstd = lax.rsqrt(jnp.sum(s2, axis=-1, keepdims=True) * (1.0 / D_INNER) + LN_EPS)
    for j in range(n_lane_blocks):
        sl = slice(j * LANES, (j + 1) * LANES)
        c = (conv_ref[j] - mu) * rstd * lng_ref[:, sl] + lnb_ref[:, sl]
        v_ref[:, sl] = (_silu(c) * sg_ref[:, sl]).astype(BF16)

    y = jnp.dot(v_ref[...], wout_ref[...], preferred_element_type=F32)
    o_ref[0] = _res_ln(xc, y, g2_ref[...], b2_ref[...])


def _conv_layer(x, w_in, conv_w, conv_b, ln_g, ln_b, w_out, g2, b2):
    bsz, seq, d = x.shape
    tl, halo = CONV_TL, CONV_HALO
    n_cb = D_INNER // CONV_CB
    n_lb = D_INNER // LANES
    a, b, g = jnp.split(w_in.astype(BF16), 3, axis=-1)
    wab = jnp.concatenate([a.reshape(d, n_cb, CONV_CB), b.reshape(d, n_cb, CONV_CB)], axis=-1)
    wab = wab.transpose(1, 0, 2)
    wg = g.reshape(d, n_cb, CONV_CB).transpose(1, 0, 2)
    cw = jnp.pad(conv_w, ((0, 1), (0, 0))).reshape(CONV_WIDTH + 1, n_lb, LANES).transpose(1, 0, 2)
    cb = conv_b.reshape(n_lb, 1, LANES)
    hpt = tl // halo
    last_halo = seq // halo - 1
    return pl.pallas_call(
        _conv_layer_kernel,
        grid=(bsz, seq // tl),
        in_specs=[
            pl.BlockSpec((1, halo, d), lambda i, t: (i, jnp.maximum(t * hpt - 1, 0), 0)),
            pl.BlockSpec((1, tl, d), lambda i, t: (i, t, 0)),
            pl.BlockSpec((1, halo, d), lambda i, t: (i, jnp.minimum((t + 1) * hpt, last_halo), 0)),
            _const_spec((n_cb, d, 2 * CONV_CB)),
            _const_spec((n_cb, d, CONV_CB)),
            _const_spec((n_lb, CONV_WIDTH + 1, LANES)),
            _const_spec((n_lb, 1, LANES)),
            _const_spec((1, D_INNER)),
            _const_spec((1, D_INNER)),
            _const_spec((D_INNER, d)),
            _const_spec((1, d)),
            _const_spec((1, d)),
        ],
        out_specs=pl.BlockSpec((1, tl, d), lambda i, t: (i, t, 0)),
        out_shape=jax.ShapeDtypeStruct((bsz, seq, d), F32),
        scratch_shapes=[
            pltpu.VMEM((n_lb, tl + 2 * halo, LANES), F32),
            pltpu.VMEM((n_lb, tl, LANES), F32),
            pltpu.VMEM((tl, D_INNER), F32),
            pltpu.VMEM((tl, D_INNER), BF16),
        ],
        compiler_params=pltpu.CompilerParams(dimension_semantics=("arbitrary", "arbitrary"),
                                             vmem_limit_bytes=VMEM_LIMIT),
        name="conv_layer",
    )(x, x, x, wab, wg, cw, cb, ln_g.reshape(1, -1), ln_b.reshape(1, -1), w_out.astype(BF16),
      g2.reshape(1, -1), b2.reshape(1, -1))


def _scan_masks():
    half, nb8 = SCAN_C // 2, SCAN_C // SUBLANES
    i = np.arange(half)[:, None]
    j = np.arange(half)[None, :]
    same8 = (i % nb8) == (j % nb8)
    masks = [same8 & (i // (nb8 << lvl) == j // (nb8 << lvl)) for lvl in range(3)]
    masks += [i // (SUBLANES << lvl) == j // (SUBLANES << lvl) for lvl in range(SCAN_LEVELS - 3)]
    return np.stack(masks).astype(np.float32)


def _gates(raw, lb):
    e_abs = jnp.exp(-jnp.abs(raw))
    r_abs = 1.0 / (1.0 + e_abs)
    er = e_abs * r_abs
    pos = raw >= 0
    c1 = 1.0 - lb
    k = c1 * jnp.where(pos, er, r_abs)
    f = lb + c1 * jnp.where(pos, r_abs, er)
    return k, jnp.log(jnp.maximum(f, F_FLOOR))


_NT = (((1,), (1,)), ((), ()))


def _level_scores(qc, kc, mask):
    s = lax.dot_general(qc.astype(BF16), kc.astype(BF16), _NT, preferred_element_type=F32)
    if mask is not None:
        s = s * mask
    return s.astype(BF16)


def _scan_chunk(q_ref, f_ref, v_ref, half, row0, lb, st_ref, tmp_ref, masks_ref, reverse):
    c, sub = SCAN_C, SUBLANES
    nb8 = c // sub

    def unpack(words):
        bits = (words << 16) if half == 0 else (words & jnp.uint32(0xFFFF0000))
        return lax.bitcast_convert_type(bits, F32)

    def piece(ref, r):
        return unpack(ref[0, pl.ds(row0 + r, nb8, stride=sub), :])

    q = [_silu(piece(q_ref, r)) for r in range(sub)]
    kp = [_gates(piece(f_ref, r), lb) for r in range(sub)]
    k = [a for a, _ in kp]
    p = [b for _, b in kp]
    v = [piece(v_ref, r) for r in range(sub)]
    o = [jnp.sum(q[r] * k[r], axis=-1, keepdims=True) * v[r] for r in range(sub)]
    yield None
    low = []
    for lvl in range(3):
        m = 1 << lvl
        q_rows, k_rows, e_of = [], [], {}
        for blk in range(0, sub, 2 * m):
            lo, hi = list(range(blk, blk + m)), list(range(blk + m, blk + 2 * m))
            q_rows += lo if reverse else hi
            k_rows += hi if reverse else lo
            for r in lo + hi:
                e_of[r] = blk + m if reverse else blk + m - 1
        qc = jnp.concatenate([q[r] * jnp.exp(p[r]) for r in q_rows], axis=0)
        kc = jnp.concatenate([k[r] if r == e_of[r] else k[r] * jnp.exp(p[e_of[r]] - p[r])
                              for r in k_rows], axis=0)
        vc = jnp.concatenate([v[r] for r in k_rows], axis=0)
        low.append((q_rows, _level_scores(qc, kc, masks_ref[lvl]), vc.astype(BF16)))
        for r in q_rows:
            p[r] = p[r] + p[e_of[r]]
    for r in range(sub):
        for slot, arr in enumerate((p, q, k)):
            tmp_ref[slot, pl.ds(r, nb8, stride=sub), :] = arr[r]
    yield None
    low = [(q_rows, jnp.dot(sc, vc, preferred_element_type=F32)) for q_rows, sc, vc in low]
    yield None
    for q_rows, ol in low:
        for i, r in enumerate(q_rows):
            o[r] = o[r] + ol[i * nb8:(i + 1) * nb8]
    for r in range(sub):
        tmp_ref[3, pl.ds(r, nb8, stride=sub), :] = o[r]
    tile = lambda x, i: x[i * sub:(i + 1) * sub]
    pn, qn, kn, vn = tmp_ref[0], tmp_ref[1], tmp_ref[2], unpack(v_ref[0, row0:row0 + c, :])
    pt = [tile(pn, i) for i in range(nb8)]
    qt = [tile(qn, i) for i in range(nb8)]
    kt = [tile(kn, i) for i in range(nb8)]
    vt = [tile(vn, i) for i in range(nb8)]
    high = []
    for lvl in range(3, SCAN_LEVELS):
        mt = (1 << lvl) // sub
        q_tiles, k_tiles, eb = [], [], {}
        for blk in range(0, nb8, 2 * mt):
            lo, hi = list(range(blk, blk + mt)), list(range(blk + mt, blk + 2 * mt))
            q_tiles += lo if reverse else hi
            k_tiles += hi if reverse else lo
            e = pt[blk + mt][0:1, :] if reverse else pt[blk + mt - 1][sub - 1:sub, :]
            e = jnp.broadcast_to(e, (sub, LANES))
            for i in lo + hi:
                eb[i] = e
        qc = jnp.concatenate([qt[i] * jnp.exp(pt[i]) for i in q_tiles], axis=0)
        kc = jnp.concatenate([kt[i] * jnp.exp(eb[i] - pt[i]) for i in k_tiles], axis=0)
        vc = jnp.concatenate([vt[i] for i in k_tiles], axis=0)
        high.append((q_tiles, _level_scores(qc, kc, masks_ref[lvl] if lvl < SCAN_LEVELS - 1 else None),
                     vc.astype(BF16)))
        for i in q_tiles:
            pt[i] = pt[i] + eb[i]
    yield None
    pn = jnp.concatenate(pt, axis=0)
    edge = pt[0][0:1, :] if reverse else pt[nb8 - 1][sub - 1:sub, :]
    st = st_ref[...]
    o_inter = lax.dot_general((qn * jnp.exp(pn)).astype(BF16), st.astype(BF16), _NT,
                              preferred_element_type=F32)
    k_dec = (kn * jnp.exp(edge - pn)).astype(BF16)
    st_ref[...] = jnp.exp(edge) * st + lax.dot_general(
        vn.astype(BF16), k_dec, (((0,), (0,)), ((), ())), preferred_element_type=F32)
    high = [(q_tiles, jnp.dot(sc, vc, preferred_element_type=F32)) for q_tiles, sc, vc in high]
    yield None
    on = tmp_ref[3]
    ot = [tile(on, i) for i in range(nb8)]
    for q_tiles, ol in high:
        for n, i in enumerate(q_tiles):
            ot[i] = ot[i] + ol[n * sub:(n + 1) * sub]
    yield jnp.concatenate(ot, axis=0) + o_inter


def _hgrn_scan_kernel(layer, *refs):
    hb = SCAN_HB
    in_refs = refs[:6]
    lb_ref, masks_ref, of_ref, ob_ref, st_ref = refs[6:11]
    tmp_refs = refs[11:]

    @pl.when(pl.program_id(2) == 0)
    def _():
        st_ref[...] = jnp.zeros_like(st_ref)

    z = lb_ref[...]
    ez = jnp.exp(z - jnp.max(z, axis=1, keepdims=True))
    pz = ez / jnp.sum(ez, axis=1, keepdims=True)
    lb = jnp.sum(pz[:, :layer + 1, :], axis=1) - pz[:, 0, :]

    streams = []
    for step in range(SCAN_CH):
        for d, o_ref in enumerate((of_ref, ob_ref)):
            ci = step if d == 0 else SCAN_CH - 1 - step
            for j in range(hb):
                sl = slice(j * LANES, (j + 1) * LANES)
                q_ref, f_ref, v_ref = in_refs[3 * d:3 * d + 3]
                gen = _scan_chunk(q_ref, f_ref, v_ref, j, ci * SCAN_C, lb[d:d + 1, sl], st_ref.at[d, j],
                                  tmp_refs[len(streams)], masks_ref, d == 1)
                streams.append((o_ref, ci * SCAN_C, sl, gen))
    outs = [None] * len(streams)
    for _ in range(SCAN_PHASES):
        for n, stream in enumerate(streams):
            outs[n] = next(stream[3])
    for (o_ref, row0, sl, _), out in zip(streams, outs):
        o_ref[0, row0:row0 + SCAN_C, sl] = out


def _hgrn_scan(h, hgrn_lb, layer):
    bsz, seq, _ = h.shape
    c, hb = SCAN_C * SCAN_CH, SCAN_HB
    assert hb == 2
    nc = seq // c
    n_pairs = N_HEADS // hb

    def col(part, rev):
        if rev:
            return pl.BlockSpec((1, c, LANES), lambda i, hg, n: (i, nc - 1 - n, part * n_pairs + hg))
        return pl.BlockSpec((1, c, LANES), lambda i, hg, n: (i, n, part * n_pairs + hg))

    in_specs = [col(0, False), col(1, False), col(3, False), col(0, True), col(2, True), col(3, True)]
    in_specs += [pl.BlockSpec((2, DEPTH, hb * LANES), lambda i, hg, n: (0, 0, hg)),
                 _const_spec((SCAN_LEVELS, SCAN_C // 2, SCAN_C // 2))]
    out_f = pl.BlockSpec((1, c, hb * LANES), lambda i, hg, n: (i, n, hg))
    out_b = pl.BlockSpec((1, c, hb * LANES), lambda i, hg, n: (i, nc - 1 - n, hg))
    return pl.pallas_call(
        functools.partial(_hgrn_scan_kernel, layer),
        grid=(bsz, N_HEADS // hb, nc),
        in_specs=in_specs,
        out_specs=[out_f, out_b],
        out_shape=[jax.ShapeDtypeStruct((bsz, seq, D_INNER), F32)] * 2,
        scratch_shapes=[pltpu.VMEM((2, hb, HEAD_DIM, HEAD_DIM), F32)]
        + [pltpu.VMEM((4, SCAN_C, LANES), F32)] * (2 * hb * SCAN_CH),
        compiler_params=pltpu.CompilerParams(
            dimension_semantics=("arbitrary", "arbitrary", "arbitrary"), vmem_limit_bytes=VMEM_LIMIT),
        name="hgrn_scan",
    )(h, h, h, h, h, h, hgrn_lb, jnp.asarray(_scan_masks()))


def _hgrn_out_kernel(of_ref, ob_ref, x_ref, wg_ref, ng_ref, wout_ref, g2_ref, b2_ref, o_ref, v_ref):
    x = x_ref[...]
    gate = _silu(jnp.dot(x.astype(BF16), wg_ref[...], preferred_element_type=F32))
    for hd in range(N_HEADS):
        sl = slice(hd * HEAD_DIM, (hd + 1) * HEAD_DIM)
        o = of_ref[:, sl] + ob_ref[:, sl]
        o = o * lax.rsqrt(jnp.mean(o * o, axis=-1, keepdims=True) + LN_EPS)
        v_ref[:, sl] = (o * ng_ref[:, sl] * gate[:, sl]).astype(BF16)
    y = jnp.dot(v_ref[...], wout_ref[...], preferred_element_type=F32)
    o_ref[...] = _res_ln(x, y, g2_ref[...], b2_ref[...])


def _hgrn_out(o_fw, o_bw, x2d, w_gate, norm_g, w_out, g2, b2):
    t, d = x2d.shape
    tm = OUT_TM
    row = lambda i: (i, 0)
    return pl.pallas_call(
        _hgrn_out_kernel,
        grid=(t // tm,),
        in_specs=[pl.BlockSpec((tm, D_INNER), row), pl.BlockSpec((tm, D_INNER), row),
                  pl.BlockSpec((tm, d), row), _const_spec((d, D_INNER)),
                  _const_spec((1, D_INNER)), _const_spec((D_INNER, d)),
                  _const_spec((1, d)), _const_spec((1, d))],
        out_specs=pl.BlockSpec((tm, d), row),
        out_shape=jax.ShapeDtypeStruct((t, d), F32),
        scratch_shapes=[pltpu.VMEM((tm, D_INNER), BF16)],
        compiler_params=pltpu.CompilerParams(dimension_semantics=("arbitrary",),
                                             vmem_limit_bytes=VMEM_LIMIT),
        name="hgrn_out",
    )(o_fw, o_bw, x2d, w_gate, norm_g.reshape(1, -1), w_out.astype(BF16), g2.reshape(1, -1),
      b2.reshape(1, -1))


def _hgrn_layer(x, w_in, hgrn_lb, layer, norm_g, w_out, g2, b2):
    bsz, seq, d = x.shape
    x2d = x.reshape(bsz * seq, d)
    w_bf = w_in.astype(BF16)
    per_tile = PROJ_TN // (2 * HEAD_DIM)
    w4 = w_bf[:, :4 * D_INNER].reshape(d, 4, N_HEADS // (2 * per_tile), per_tile, 2, HEAD_DIM)
    w4 = w4.transpose(0, 1, 2, 4, 3, 5).reshape(d, 4 * D_INNER)
    h = _in_proj_pairs(x2d, w4)
    o_fw, o_bw = _hgrn_scan(h.reshape(bsz, seq, -1), hgrn_lb, layer)
    y = _hgrn_out(o_fw.reshape(bsz * seq, -1), o_bw.reshape(bsz * seq, -1), x2d, w_bf[:, 4 * D_INNER:],
                  norm_g, w_out, g2, b2)
    return y.reshape(bsz, seq, d)


def _t5_bucket(rel):
    nb = REL_BUCKETS // 2
    max_exact = nb // 2
    ret = (rel > 0).astype(jnp.int32) * nb
    n = jnp.abs(rel)
    large = max_exact + (jnp.log(jnp.maximum(n, 1).astype(jnp.float32) / max_exact)
                         / math.log(REL_MAX_DIST / max_exact) * (nb - max_exact)).astype(jnp.int32)
    large = jnp.minimum(large, nb - 1)
    return ret + jnp.where(n < max_exact, n, large)


def _attn_layer_kernel(q_ref, kp_ref, kc_ref, kn_ref, vp_ref, vc_ref, vn_ref, x_ref,
                       bucket_ref, relb_ref, sink_ref, wg_ref, wout_ref, g2_ref, b2_ref, o_ref,
                       bias_ref, v_ref):
    n = pl.program_id(1)
    inv_scale = 1.0 / ATTN_SCALE
    neg_inf = -jnp.inf

    @pl.when((pl.program_id(0) == 0) & (n == 0))
    def _():
        bucket = bucket_ref[...]
        col = lax.broadcasted_iota(jnp.int32, bucket.shape, 1)
        row = lax.broadcasted_iota(jnp.int32, bucket.shape, 0)
        in_band = jnp.abs(col - BLOCK - row) <= WINDOW
        for h in range(N_HEADS):
            acc = jnp.zeros(bucket.shape, F32)
            for b in range(REL_BUCKETS):
                acc = jnp.where(bucket == b, relb_ref[b, h], acc)
            bias_ref[h] = jnp.where(in_band, acc * inv_scale, neg_inf)

    x = x_ref[0]
    gate = _silu(jnp.dot(x.astype(BF16), wg_ref[...], preferred_element_type=F32))
    kwin = jnp.concatenate([kp_ref[0], kc_ref[0], kn_ref[0]], axis=0)
    vwin = jnp.concatenate([vp_ref[0], vc_ref[0], vn_ref[0]], axis=0)
    qb = q_ref[0]
    rows = GROUP * BLOCK
    has_prev = n > 0
    has_next = n < pl.num_programs(1) - 1
    head_of_row = lax.broadcasted_iota(jnp.int32, (rows, 1), 0) // BLOCK

    scores = []
    for kh in range(KV_HEADS):
        h0 = kh * GROUP
        qg = jnp.concatenate([qb[:, (h0 + gi) * HEAD_DIM:(h0 + gi + 1) * HEAD_DIM]
                              for gi in range(GROUP)], axis=0)
        scores.append(lax.dot_general(qg, kwin[:, kh * HEAD_DIM:(kh + 1) * HEAD_DIM], _NT,
                                      preferred_element_type=F32))
    c2 = ATTN_SCALE * math.log2(math.e)
    for kh in range(KV_HEADS):
        h0 = kh * GROUP
        s = scores[kh] + bias_ref[pl.ds(h0, GROUP)].reshape(rows, 3 * BLOCK)
        s = jnp.concatenate([jnp.where(has_prev, s[:, :BLOCK], neg_inf), s[:, BLOCK:2 * BLOCK],
                             jnp.where(has_next, s[:, 2 * BLOCK:], neg_inf)], axis=1)
        sink = jnp.zeros((rows, 1), F32)
        for gi in range(GROUP):
            sink = jnp.where(head_of_row == gi, sink_ref[h0 + gi] * inv_scale, sink)
        mx = jnp.maximum(jnp.max(s, axis=-1, keepdims=True), sink)
        p = jnp.exp2((s - mx) * c2)
        den = jnp.sum(p, axis=-1, keepdims=True) + jnp.exp2((sink - mx) * c2)
        o = jnp.dot(p.astype(BF16), vwin[:, kh * HEAD_DIM:(kh + 1) * HEAD_DIM],
                    preferred_element_type=F32) / den
        for gi in range(GROUP):
            sl = slice((h0 + gi) * HEAD_DIM, (h0 + gi + 1) * HEAD_DIM)
            v_ref[:, sl] = (o[gi * BLOCK:(gi + 1) * BLOCK] * gate[:, sl]).astype(BF16)

    y = jnp.dot(v_ref[...], wout_ref[...], preferred_element_type=F32)
    o_ref[0] = _res_ln(x, y, g2_ref[...], b2_ref[...])


def _attn_layer(x, w_in, sink, rel_bias, w_out, g2, b2):
    bsz, seq, d = x.shape
    nb = seq // BLOCK
    w_bf = w_in.astype(BF16)
    qkv = _in_proj(x.reshape(bsz * seq, d), w_bf[:, :D_INNER + 2 * KV_WIDTH], BF16).reshape(bsz, seq, -1)
    q_pos = jnp.arange(BLOCK)
    k_off = jnp.arange(3 * BLOCK) - BLOCK
    bucket = _t5_bucket(k_off[None, :] - q_pos[:, None])
    k_col = D_INNER // KV_WIDTH
    v_col = k_col + 1
    prev = lambda i, n: jnp.maximum(n - 1, 0)
    nxt = lambda i, n: jnp.minimum(n + 1, nb - 1)
    kv = lambda colb, f: pl.BlockSpec((1, BLOCK, KV_WIDTH), lambda i, n: (i, f(i, n), colb))
    cur = lambda i, n: n
    return pl.pallas_call(
        _attn_layer_kernel,
        grid=(bsz, nb),
        in_specs=[
            pl.BlockSpec((1, BLOCK, D_INNER), lambda i, n: (i, n, 0)),
            kv(k_col, prev), kv(k_col, cur), kv(k_col, nxt),
            kv(v_col, prev), kv(v_col, cur), kv(v_col, nxt),
            pl.BlockSpec((1, BLOCK, d), lambda i, n: (i, n, 0)),
            _const_spec((BLOCK, 3 * BLOCK)),
            pl.BlockSpec(memory_space=pltpu.SMEM),
            pl.BlockSpec(memory_space=pltpu.SMEM),
            _const_spec((d, D_INNER)),
            _const_spec((D_INNER, d)),
            _const_spec((1, d)),
            _const_spec((1, d)),
        ],
        out_specs=pl.BlockSpec((1, BLOCK, d), lambda i, n: (i, n, 0)),
        out_shape=jax.ShapeDtypeStruct((bsz, seq, d), F32),
        scratch_shapes=[pltpu.VMEM((N_HEADS, BLOCK, 3 * BLOCK), F32),
                        pltpu.VMEM((BLOCK, D_INNER), BF16)],
        compiler_params=pltpu.CompilerParams(dimension_semantics=("arbitrary", "arbitrary"),
                                             vmem_limit_bytes=VMEM_LIMIT),
        name="attn_layer",
    )(qkv, qkv, qkv, qkv, qkv, qkv, qkv, x, bucket, rel_bias, sink, w_bf[:, D_INNER + 2 * KV_WIDTH:],
      w_out.astype(BF16), g2.reshape(1, -1), b2.reshape(1, -1))


def _trunk(x, ln_g, ln_b, w_in_conv, conv_w, conv_b, conv_ln_g, conv_ln_b, w_out_conv,
           w_in_hgrn, hgrn_lb, hgrn_norm_g, w_out_hgrn, w_in_attn, attn_sink, rel_bias, w_out_attn):
    for i in range(DEPTH):
        j = i // N_MIXERS
        kind = i % N_MIXERS
        if kind == 0:
            x = _conv_layer(x, w_in_conv[j], conv_w[j], conv_b[j], conv_ln_g[j], conv_ln_b[j],
                            w_out_conv[j], ln_g[i], ln_b[i])
        elif kind == 1:
            x = _hgrn_layer(x, w_in_hgrn[j], hgrn_lb, i, hgrn_norm_g[j], w_out_hgrn[j],
                            ln_g[i], ln_b[i])
        else:
            x = _attn_layer(x, w_in_attn[j], attn_sink[j], rel_bias, w_out_attn[j], ln_g[i], ln_b[i])
    return x


def kernel(x_prompt, x_sample, ln_g, ln_b, w_in_conv, conv_w, conv_b, conv_ln_g, conv_ln_b,
           w_out_conv, w_in_hgrn, hgrn_lb, hgrn_norm_g, w_out_hgrn, w_in_attn, attn_sink,
           rel_bias, w_out_attn):
    params = (ln_g, ln_b, w_in_conv, conv_w, conv_b, conv_ln_g, conv_ln_b, w_out_conv,
              w_in_hgrn, hgrn_lb, hgrn_norm_g, w_out_hgrn, w_in_attn, attn_sink, rel_bias, w_out_attn)
    return (_trunk(x_prompt, *params), _trunk(x_sample, *params))
```

```python
import functools
import math

import jax
import jax.numpy as jnp
import numpy as np
from jax import lax
from jax.experimental import pallas as pl
from jax.experimental.pallas import tpu as pltpu

D_MODEL = 1024
DEPTH = 4
N_MIXERS = 3
D_INNER = 2 * D_MODEL
CONV_WIDTH = 31
CONV_HALF = CONV_WIDTH // 2
HEAD_DIM = 128
N_HEADS = D_INNER // HEAD_DIM
KV_HEADS = 4
GROUP = N_HEADS // KV_HEADS
KV_WIDTH = KV_HEADS * HEAD_DIM
ATTN_SCALE = HEAD_DIM ** -0.5
WINDOW = 128
BLOCK = 128
REL_BUCKETS = 32
REL_MAX_DIST = 128
ALPHA = (2 * DEPTH) ** 0.25
LN_EPS = 1e-5

LANES = 128
SUBLANES = 8
VMEM_LIMIT = 56 * 1024 * 1024

CONV_TL = 512
CONV_HALO = 16
CONV_RC = 128
SCAN_C = 128
SCAN_LEVELS = SCAN_C.bit_length() - 1
SCAN_HB = 2
SCAN_CH = 4
SCAN_PHASES = 6
PROJ_TM = 2048
PROJ_TN = 1024
OUT_TM = 256
F_FLOOR = 1e-37

F32 = jnp.float32
BF16 = jnp.bfloat16


def _const_spec(shape):
    return pl.BlockSpec(shape, lambda *_: (0,) * len(shape), pipeline_mode=pl.Buffered(1))


def _sigmoid(x):
    return 1.0 / (1.0 + jnp.exp(-x))


def _silu(x):
    return x * _sigmoid(x)


def _res_ln(x, y, g, b):
    z = ALPHA * x + y
    mu = jnp.mean(z, axis=-1, keepdims=True)
    d = z - mu
    var = jnp.mean(d * d, axis=-1, keepdims=True)
    return d * lax.rsqrt(var + LN_EPS) * g + b


def _proj_kernel(x_ref, w_ref, o_ref):
    o_ref[...] = jnp.dot(x_ref[...].astype(BF16), w_ref[...],
                         preferred_element_type=F32).astype(o_ref.dtype)


def _in_proj(x2d, w_bf, out_dtype):
    t, d = x2d.shape
    n = w_bf.shape[1]
    return pl.pallas_call(
        _proj_kernel,
        grid=(t // PROJ_TM, n // PROJ_TN),
        in_specs=[pl.BlockSpec((PROJ_TM, d), lambda i, j: (i, 0)),
                  pl.BlockSpec((d, PROJ_TN), lambda i, j: (0, j))],
        out_specs=pl.BlockSpec((PROJ_TM, PROJ_TN), lambda i, j: (i, j)),
        out_shape=jax.ShapeDtypeStruct((t, n), out_dtype),
        compiler_params=pltpu.CompilerParams(dimension_semantics=("arbitrary", "arbitrary"),
                                             vmem_limit_bytes=VMEM_LIMIT),
        name="in_proj",
    )(x2d, w_bf)


def _proj_pair_kernel(x_ref, w_ref, o_ref):
    acc = jnp.dot(x_ref[...].astype(BF16), w_ref[...], preferred_element_type=F32)
    half = acc.shape[1] // 2
    lo = lax.bitcast_convert_type(acc[:, :half].astype(BF16).astype(F32), jnp.uint32) >> 16
    hi = lax.bitcast_convert_type(acc[:, half:].astype(BF16).astype(F32), jnp.uint32)
    o_ref[...] = lo | hi


def _in_proj_pairs(x2d, w_bf):
    t, d = x2d.shape
    n = w_bf.shape[1]
    return pl.pallas_call(
        _proj_pair_kernel,
        grid=(t // PROJ_TM, n // PROJ_TN),
        in_specs=[pl.BlockSpec((PROJ_TM, d), lambda i, j: (i, 0)),
                  pl.BlockSpec((d, PROJ_TN), lambda i, j: (0, j))],
        out_specs=pl.BlockSpec((PROJ_TM, PROJ_TN // 2), lambda i, j: (i, j)),
        out_shape=jax.ShapeDtypeStruct((t, n // 2), jnp.uint32),
        compiler_params=pltpu.CompilerParams(dimension_semantics=("arbitrary", "arbitrary"),
                                             vmem_limit_bytes=VMEM_LIMIT),
        name="in_proj_pairs",
    )(x2d, w_bf)


def _conv_stage_a(xe_ref, w3_ref, jj, prev_ok, next_ok, uext_ref, sg_ref):
    tl, halo, pair = CONV_TL, CONV_HALO, 2 * LANES
    h = jnp.dot(xe_ref[...], w3_ref[jj], preferred_element_type=F32)
    u = h[:, :pair] * _sigmoid(h[:, pair:2 * pair])
    u = jnp.concatenate([jnp.where(prev_ok, u[:halo], 0.0), u[halo:halo + tl],
                         jnp.where(next_ok, u[halo + tl:], 0.0)], axis=0)
    sg = _silu(h[halo:halo + tl, 2 * pair:])
    for i in range(2):
        uext_ref[2 * jj + i] = u[:, i * LANES:(i + 1) * LANES]
        sg_ref[2 * jj + i] = sg[:, i * LANES:(i + 1) * LANES]


def _conv_layer_kernel(nt, xpn_ref, xcn_ref, xnn_ref, xc_ref, w3_ref, cw_ref, cb_ref, lng_ref, lnb_ref,
                       wout_ref, g2_ref, b2_ref, o_ref, xe_ref, uext0_ref, uext1_ref, sg0_ref, sg1_ref,
                       conv_ref, v_ref):
    tl, halo = CONV_TL, CONV_HALO
    n_lane_blocks = D_INNER // LANES
    s = pl.program_id(0)

    @pl.when(s == 0)
    def _():
        xe_ref[...] = jnp.concatenate([jnp.zeros((halo, D_MODEL), F32), xc_ref[0], xcn_ref[0, :halo]],
                                      axis=0).astype(BF16)

        def first(jj, carry):
            _conv_stage_a(xe_ref, w3_ref, jj, False, True, uext0_ref, sg0_ref)
            return carry

        lax.fori_loop(0, n_lane_blocks // 2, first, 0)

    t1 = jnp.minimum(s + 1, pl.num_programs(0) - 1) % nt
    prev_ok = t1 > 0
    next_ok = t1 < nt - 1

    def step(uext_ref, sg_ref, uext_next_ref, sg_next_ref):
        def lane_block(j, carry):
            w = cw_ref[j]
            bias = cb_ref[j]
            for rc in range(tl // CONV_RC):
                acc = jnp.broadcast_to(bias, (CONV_RC, LANES))
                for k in range(CONV_WIDTH):
                    start = rc * CONV_RC + k + halo - CONV_HALF
                    acc = acc + w[k:k + 1, :] * uext_ref[j, pl.ds(start, CONV_RC), :]
                conv_ref[j, pl.ds(rc * CONV_RC, CONV_RC), :] = acc
            return carry

        lax.fori_loop(0, n_lane_blocks, lane_block, 0)

        xe_ref[...] = jnp.concatenate([xpn_ref[0], xcn_ref[0], xnn_ref[0]], axis=0).astype(BF16)
        for jj in range(n_lane_blocks // 2):
            _conv_stage_a(xe_ref, w3_ref, jj, prev_ok, next_ok, uext_next_ref, sg_next_ref)

        s1 = conv_ref[0]
        for j in range(1, n_lane_blocks):
            s1 = s1 + conv_ref[j]
        mu = jnp.sum(s1, axis=-1, keepdims=True) * (1.0 / D_INNER)
        s2 = jnp.zeros((tl, LANES), F32)
        for j in range(n_lane_blocks):
            d = conv_ref[j] - mu
            s2 = s2 + d * d
        rstd = lax.rsqrt(jnp.sum(s2, axis=-1, keepdims=True) * (1.0 / D_INNER) + LN_EPS)
        for j in range(n_lane_blocks):
            sl = slice(j * LANES, (j + 1) * LANES)
            c = (conv_ref[j] - mu) * rstd * lng_ref[:, sl] + lnb_ref[:, sl]
            v_ref[:, sl] = (_silu(c) * sg_ref[j]).astype(BF16)

        y = jnp.dot(v_ref[...], wout_ref[...], preferred_element_type=F32)
        o_ref[0] = _res_ln(xc_ref[0], y, g2_ref[...], b2_ref[...])

    @pl.when(s % 2 == 0)
    def _():
        step(uext0_ref, sg0_ref, uext1_ref, sg1_ref)

    @pl.when(s % 2 == 1)
    def _():
        step(uext1_ref, sg1_ref, uext0_ref, sg0_ref)


def _conv_layer(x, w_in, conv_w, conv_b, ln_g, ln_b, w_out, g2, b2):
    bsz, seq, d = x.shape
    tl, halo = CONV_TL, CONV_HALO
    n_lb = D_INNER // LANES
    n_pair = n_lb // 2
    nt = seq // tl
    assert nt > 1
    steps = bsz * nt
    a, b, g = jnp.split(w_in.astype(BF16), 3, axis=-1)
    w3 = jnp.concatenate([p.reshape(d, n_pair, 2 * LANES) for p in (a, b, g)], axis=-1)
    w3 = w3.transpose(1, 0, 2)
    cw = jnp.pad(conv_w, ((0, 1), (0, 0))).reshape(CONV_WIDTH + 1, n_lb, LANES).transpose(1, 0, 2)
    cb = conv_b.reshape(n_lb, 1, LANES)
    hpt = tl // halo
    last_halo = seq // halo - 1

    def nxt(s):
        s1 = jnp.minimum(s + 1, steps - 1)
        return s1 // nt, s1 % nt

    def prev_halo(s):
        b1, t1 = nxt(s)
        return b1, jnp.maximum(t1 * hpt - 1, 0), 0

    def next_halo(s):
        b1, t1 = nxt(s)
        return b1, jnp.minimum((t1 + 1) * hpt, last_halo), 0

    return pl.pallas_call(
        functools.partial(_conv_layer_kernel, nt),
        grid=(steps,),
        in_specs=[
            pl.BlockSpec((1, halo, d), prev_halo),
            pl.BlockSpec((1, tl, d), lambda s: (*nxt(s), 0)),
            pl.BlockSpec((1, halo, d), next_halo),
            pl.BlockSpec((1, tl, d), lambda s: (s // nt, s % nt, 0)),
            _const_spec((n_pair, d, 6 * LANES)),
            _const_spec((n_lb, CONV_WIDTH + 1, LANES)),
            _const_spec((n_lb, 1, LANES)),
            _const_spec((1, D_INNER)),
            _const_spec((1, D_INNER)),
            _const_spec((D_INNER, d)),
            _const_spec((1, d)),
            _const_spec((1, d)),
        ],
        out_specs=pl.BlockSpec((1, tl, d), lambda s: (s // nt, s % nt, 0)),
        out_shape=jax.ShapeDtypeStruct((bsz, seq, d), F32),
        scratch_shapes=[
            pltpu.VMEM((tl + 2 * halo, d), BF16),
            pltpu.VMEM((n_lb, tl + 2 * halo, LANES), F32),
            pltpu.VMEM((n_lb, tl + 2 * halo, LANES), F32),
            pltpu.VMEM((n_lb, tl, LANES), F32),
            pltpu.VMEM((n_lb, tl, LANES), F32),
            pltpu.VMEM((n_lb, tl, LANES), F32),
            pltpu.VMEM((tl, D_INNER), BF16),
        ],
        compiler_params=pltpu.CompilerParams(dimension_semantics=("arbitrary",),
                                             vmem_limit_bytes=VMEM_LIMIT),
        name="conv_layer",
    )(x, x, x, x, w3, cw, cb, ln_g.reshape(1, -1), ln_b.reshape(1, -1), w_out.astype(BF16),
      g2.reshape(1, -1), b2.reshape(1, -1))


def _scan_masks():
    half, nb8 = SCAN_C // 2, SCAN_C // SUBLANES
    i = np.arange(half)[:, None]
    j = np.arange(half)[None, :]
    same8 = (i % nb8) == (j % nb8)
    masks = [same8 & (i // (nb8 << lvl) == j // (nb8 << lvl)) for lvl in range(3)]
    masks += [i // (SUBLANES << lvl) == j // (SUBLANES << lvl) for lvl in range(SCAN_LEVELS - 3)]
    return np.stack(masks).astype(np.float32)


def _gates(raw, lb):
    e_abs = jnp.exp(-jnp.abs(raw))
    r_abs = 1.0 / (1.0 + e_abs)
    er = e_abs * r_abs
    pos = raw >= 0
    c1 = 1.0 - lb
    k = c1 * jnp.where(pos, er, r_abs)
    f = lb + c1 * jnp.where(pos, r_abs, er)
    return k, jnp.log(jnp.maximum(f, F_FLOOR))


_NT = (((1,), (1,)), ((), ()))


def _level_scores(qc, kc, mask):
    s = lax.dot_general(qc.astype(BF16), kc.astype(BF16), _NT, preferred_element_type=F32)
    if mask is not None:
        s = s * mask
    return s.astype(BF16)


def _scan_chunk(q_ref, f_ref, v_ref, half, row0, lb, st_ref, tmp_ref, masks_ref, reverse):
    c, sub = SCAN_C, SUBLANES
    nb8 = c // sub

    def unpack(words):
        bits = (words << 16) if half == 0 else (words & jnp.uint32(0xFFFF0000))
        return lax.bitcast_convert_type(bits, F32)

    def piece(ref, r):
        return unpack(ref[0, pl.ds(row0 + r, nb8, stride=sub), :])

    q = [_silu(piece(q_ref, r)) for r in range(sub)]
    kp = [_gates(piece(f_ref, r), lb) for r in range(sub)]
    k = [a for a, _ in kp]
    p = [b for _, b in kp]
    v = [piece(v_ref, r) for r in range(sub)]
    o = [jnp.sum(q[r] * k[r], axis=-1, keepdims=True) * v[r] for r in range(sub)]
    yield None
    low = []
    for lvl in range(3):
        m = 1 << lvl
        q_rows, k_rows, e_of = [], [], {}
        for blk in range(0, sub, 2 * m):
            lo, hi = list(range(blk, blk + m)), list(range(blk + m, blk + 2 * m))
            q_rows += lo if reverse else hi
            k_rows += hi if reverse else lo
            for r in lo + hi:
                e_of[r] = blk + m if reverse else blk + m - 1
        qc = jnp.concatenate([q[r] * jnp.exp(p[r]) for r in q_rows], axis=0)
        kc = jnp.concatenate([k[r] if r == e_of[r] else k[r] * jnp.exp(p[e_of[r]] - p[r])
                              for r in k_rows], axis=0)
        vc = jnp.concatenate([v[r] for r in k_rows], axis=0)
        low.append((q_rows, _level_scores(qc, kc, masks_ref[lvl]), vc.astype(BF16)))
        for r in q_rows:
            p[r] = p[r] + p[e_of[r]]
    for r in range(sub):
        for slot, arr in enumerate((p, q, k)):
            tmp_ref[slot, pl.ds(r, nb8, stride=sub), :] = arr[r]
    yield None
    low = [(q_rows, jnp.dot(sc, vc, preferred_element_type=F32)) for q_rows, sc, vc in low]
    yield None
    for q_rows, ol in low:
        for i, r in enumerate(q_rows):
            o[r] = o[r] + ol[i * nb8:(i + 1) * nb8]
    for r in range(sub):
        tmp_ref[3, pl.ds(r, nb8, stride=sub), :] = o[r]
    tile = lambda x, i: x[i * sub:(i + 1) * sub]
    pn, qn, kn, vn = tmp_ref[0], tmp_ref[1], tmp_ref[2], unpack(v_ref[0, row0:row0 + c, :])
    pt = [tile(pn, i) for i in range(nb8)]
    qt = [tile(qn, i) for i in range(nb8)]
    kt = [tile(kn, i) for i in range(nb8)]
    vt = [tile(vn, i) for i in range(nb8)]
    high = []
    for lvl in range(3, SCAN_LEVELS):
        mt = (1 << lvl) // sub
        q_tiles, k_tiles, eb = [], [], {}
        for blk in range(0, nb8, 2 * mt):
            lo, hi = list(range(blk, blk + mt)), list(range(blk + mt, blk + 2 * mt))
            q_tiles += lo if reverse else hi
            k_tiles += hi if reverse else lo
            e = pt[blk + mt][0:1, :] if reverse else pt[blk + mt - 1][sub - 1:sub, :]
            e = jnp.broadcast_to(e, (sub, LANES))
            for i in lo + hi:
                eb[i] = e
        qc = jnp.concatenate([qt[i] * jnp.exp(pt[i]) for i in q_tiles], axis=0)
        kc = jnp.concatenate([kt[i] * jnp.exp(eb[i] - pt[i]) for i in k_tiles], axis=0)
        vc = jnp.concatenate([vt[i] for i in k_tiles], axis=0)
        high.append((q_tiles, _level_scores(qc, kc, masks_ref[lvl] if lvl < SCAN_LEVELS - 1 else None),
                     vc.astype(BF16)))
        for i in q_tiles:
            pt[i] = pt[i] + eb[i]
    yield None
    pn = jnp.concatenate(pt, axis=0)
    edge = pt[0][0:1, :] if reverse else pt[nb8 - 1][sub - 1:sub, :]
    st = st_ref[...]
    o_inter = lax.dot_general((qn * jnp.exp(pn)).astype(BF16), st.astype(BF16), _NT,
                              preferred_element_type=F32)
    k_dec = (kn * jnp.exp(edge - pn)).astype(BF16)
    st_ref[...] = jnp.exp(edge) * st + lax.dot_general(
        vn.astype(BF16), k_dec, (((0,), (0,)), ((), ())), preferred_element_type=F32)
    high = [(q_tiles, jnp.dot(sc, vc, preferred_element_type=F32)) for q_tiles, sc, vc in high]
    yield None
    on = tmp_ref[3]
    ot = [tile(on, i) for i in range(nb8)]
    for q_tiles, ol in high:
        for n, i in enumerate(q_tiles):
            ot[i] = ot[i] + ol[n * sub:(n + 1) * sub]
    yield jnp.concatenate(ot, axis=0) + o_inter


def _hgrn_scan_kernel(layer, *refs):
    hb = SCAN_HB
    in_refs = refs[:6]
    lb_ref, masks_ref, of_ref, ob_ref, st_ref = refs[6:11]
    tmp_refs = refs[11:]

    @pl.when(pl.program_id(2) == 0)
    def _():
        st_ref[...] = jnp.zeros_like(st_ref)

    z = lb_ref[...]
    ez = jnp.exp(z - jnp.max(z, axis=1, keepdims=True))
    pz = ez / jnp.sum(ez, axis=1, keepdims=True)
    lb = jnp.sum(pz[:, :layer + 1, :], axis=1) - pz[:, 0, :]

    streams = []
    for step in range(SCAN_CH):
        for d, o_ref in enumerate((of_ref, ob_ref)):
            ci = step if d == 0 else SCAN_CH - 1 - step
            for j in range(hb):
                sl = slice(j * LANES, (j + 1) * LANES)
                q_ref, f_ref, v_ref = in_refs[3 * d:3 * d + 3]
                gen = _scan_chunk(q_ref, f_ref, v_ref, j, ci * SCAN_C, lb[d:d + 1, sl], st_ref.at[d, j],
                                  tmp_refs[len(streams)], masks_ref, d == 1)
                streams.append((o_ref, ci * SCAN_C, sl, gen))
    outs = [None] * len(streams)
    for _ in range(SCAN_PHASES):
        for n, stream in enumerate(streams):
            outs[n] = next(stream[3])
    for (o_ref, row0, sl, _), out in zip(streams, outs):
        o_ref[0, row0:row0 + SCAN_C, sl] = out


def _hgrn_scan(h, hgrn_lb, layer):
    bsz, seq, _ = h.shape
    c, hb = SCAN_C * SCAN_CH, SCAN_HB
    assert hb == 2
    nc = seq // c
    n_pairs = N_HEADS // hb

    def col(part, rev):
        if rev:
            return pl.BlockSpec((1, c, LANES), lambda i, hg, n: (i, nc - 1 - n, part * n_pairs + hg))
        return pl.BlockSpec((1, c, LANES), lambda i, hg, n: (i, n, part * n_pairs + hg))

    in_specs = [col(0, False), col(1, False), col(3, False), col(0, True), col(2, True), col(3, True)]
    in_specs += [pl.BlockSpec((2, DEPTH, hb * LANES), lambda i, hg, n: (0, 0, hg)),
                 _const_spec((SCAN_LEVELS, SCAN_C // 2, SCAN_C // 2))]
    out_f = pl.BlockSpec((1, c, hb * LANES), lambda i, hg, n: (i, n, hg))
    out_b = pl.BlockSpec((1, c, hb * LANES), lambda i, hg, n: (i, nc - 1 - n, hg))
    return pl.pallas_call(
        functools.partial(_hgrn_scan_kernel, layer),
        grid=(bsz, N_HEADS // hb, nc),
        in_specs=in_specs,
        out_specs=[out_f, out_b],
        out_shape=[jax.ShapeDtypeStruct((bsz, seq, D_INNER), F32)] * 2,
        scratch_shapes=[pltpu.VMEM((2, hb, HEAD_DIM, HEAD_DIM), F32)]
        + [pltpu.VMEM((4, SCAN_C, LANES), F32)] * (2 * hb * SCAN_CH),
        compiler_params=pltpu.CompilerParams(
            dimension_semantics=("arbitrary", "arbitrary", "arbitrary"), vmem_limit_bytes=VMEM_LIMIT),
        name="hgrn_scan",
    )(h, h, h, h, h, h, hgrn_lb, jnp.asarray(_scan_masks()))


def _hgrn_out_kernel(of_ref, ob_ref, x_ref, wg_ref, ng_ref, wout_ref, g2_ref, b2_ref, o_ref, v_ref):
    x = x_ref[...]
    gate = _silu(jnp.dot(x.astype(BF16), wg_ref[...], preferred_element_type=F32))
    for hd in range(N_HEADS):
        sl = slice(hd * HEAD_DIM, (hd + 1) * HEAD_DIM)
        o = of_ref[:, sl] + ob_ref[:, sl]
        o = o * lax.rsqrt(jnp.mean(o * o, axis=-1, keepdims=True) + LN_EPS)
        v_ref[:, sl] = (o * ng_ref[:, sl] * gate[:, sl]).astype(BF16)
    y = jnp.dot(v_ref[...], wout_ref[...], preferred_element_type=F32)
    o_ref[...] = _res_ln(x, y, g2_ref[...], b2_ref[...])


def _hgrn_out(o_fw, o_bw, x2d, w_gate, norm_g, w_out, g2, b2):
    t, d = x2d.shape
    tm = OUT_TM
    row = lambda i: (i, 0)
    return pl.pallas_call(
        _hgrn_out_kernel,
        grid=(t // tm,),
        in_specs=[pl.BlockSpec((tm, D_INNER), row), pl.BlockSpec((tm, D_INNER), row),
                  pl.BlockSpec((tm, d), row), _const_spec((d, D_INNER)),
                  _const_spec((1, D_INNER)), _const_spec((D_INNER, d)),
                  _const_spec((1, d)), _const_spec((1, d))],
        out_specs=pl.BlockSpec((tm, d), row),
        out_shape=jax.ShapeDtypeStruct((t, d), F32),
        scratch_shapes=[pltpu.VMEM((tm, D_INNER), BF16)],
        compiler_params=pltpu.CompilerParams(dimension_semantics=("arbitrary",),
                                             vmem_limit_bytes=VMEM_LIMIT),
        name="hgrn_out",
    )(o_fw, o_bw, x2d, w_gate, norm_g.reshape(1, -1), w_out.astype(BF16), g2.reshape(1, -1),
      b2.reshape(1, -1))


def _hgrn_layer(x, w_in, hgrn_lb, layer, norm_g, w_out, g2, b2):
    bsz, seq, d = x.shape
    x2d = x.reshape(bsz * seq, d)
    w_bf = w_in.astype(BF16)
    per_tile = PROJ_TN // (2 * HEAD_DIM)
    w4 = w_bf[:, :4 * D_INNER].reshape(d, 4, N_HEADS // (2 * per_tile), per_tile, 2, HEAD_DIM)
    w4 = w4.transpose(0, 1, 2, 4, 3, 5).reshape(d, 4 * D_INNER)
    h = _in_proj_pairs(x2d, w4)
    o_fw, o_bw = _hgrn_scan(h.reshape(bsz, seq, -1), hgrn_lb, layer)
    y = _hgrn_out(o_fw.reshape(bsz * seq, -1), o_bw.reshape(bsz * seq, -1), x2d, w_bf[:, 4 * D_INNER:],
                  norm_g, w_out, g2, b2)
    return y.reshape(bsz, seq, d)


def _t5_bucket(rel):
    nb = REL_BUCKETS // 2
    max_exact = nb // 2
    ret = (rel > 0).astype(jnp.int32) * nb
    n = jnp.abs(rel)
    large = max_exact + (jnp.log(jnp.maximum(n, 1).astype(jnp.float32) / max_exact)
                         / math.log(REL_MAX_DIST / max_exact) * (nb - max_exact)).astype(jnp.int32)
    large = jnp.minimum(large, nb - 1)
    return ret + jnp.where(n < max_exact, n, large)


def _attn_layer_kernel(q_ref, kp_ref, kc_ref, kn_ref, vp_ref, vc_ref, vn_ref, x_ref,
                       bucket_ref, relb_ref, sink_ref, wg_ref, wout_ref, g2_ref, b2_ref, o_ref,
                       bias_ref, v_ref):
    n = pl.program_id(1)
    inv_scale = 1.0 / ATTN_SCALE
    neg_inf = -jnp.inf

    @pl.when((pl.program_id(0) == 0) & (n == 0))
    def _():
        bucket = bucket_ref[...]
        col = lax.broadcasted_iota(jnp.int32, bucket.shape, 1)
        row = lax.broadcasted_iota(jnp.int32, bucket.shape, 0)
        in_band = jnp.abs(col - BLOCK - row) <= WINDOW
        for h in range(N_HEADS):
            acc = jnp.zeros(bucket.shape, F32)
            for b in range(REL_BUCKETS):
                acc = jnp.where(bucket == b, relb_ref[b, h], acc)
            bias_ref[h] = jnp.where(in_band, acc * inv_scale, neg_inf)

    x = x_ref[0]
    gate = _silu(jnp.dot(x.astype(BF16), wg_ref[...], preferred_element_type=F32))
    kwin = jnp.concatenate([kp_ref[0], kc_ref[0], kn_ref[0]], axis=0)
    vwin = jnp.concatenate([vp_ref[0], vc_ref[0], vn_ref[0]], axis=0)
    qb = q_ref[0]
    rows = GROUP * BLOCK
    has_prev = n > 0
    has_next = n < pl.num_programs(1) - 1
    head_of_row = lax.broadcasted_iota(jnp.int32, (rows, 1), 0) // BLOCK

    scores = []
    for kh in range(KV_HEADS):
        h0 = kh * GROUP
        qg = jnp.concatenate([qb[:, (h0 + gi) * HEAD_DIM:(h0 + gi + 1) * HEAD_DIM]
                              for gi in range(GROUP)], axis=0)
        scores.append(lax.dot_general(qg, kwin[:, kh * HEAD_DIM:(kh + 1) * HEAD_DIM], _NT,
                                      preferred_element_type=F32))
    c2 = ATTN_SCALE * math.log2(math.e)
    for kh in range(KV_HEADS):
        h0 = kh * GROUP
        s = scores[kh] + bias_ref[pl.ds(h0, GROUP)].reshape(rows, 3 * BLOCK)
        s = jnp.concatenate([jnp.where(has_prev, s[:, :BLOCK], neg_inf), s[:, BLOCK:2 * BLOCK],
                             jnp.where(has_next, s[:, 2 * BLOCK:], neg_inf)], axis=1)
        sink = jnp.zeros((rows, 1), F32)
        for gi in range(GROUP):
            sink = jnp.where(head_of_row == gi, sink_ref[h0 + gi] * inv_scale, sink)
        mx = jnp.maximum(jnp.max(s, axis=-1, keepdims=True), sink)
        p = jnp.exp2((s - mx) * c2)
        den = jnp.sum(p, axis=-1, keepdims=True) + jnp.exp2((sink - mx) * c2)
        o = jnp.dot(p.astype(BF16), vwin[:, kh * HEAD_DIM:(kh + 1) * HEAD_DIM],
                    preferred_element_type=F32) / den
        for gi in range(GROUP):
            sl = slice((h0 + gi) * HEAD_DIM, (h0 + gi + 1) * HEAD_DIM)
            v_ref[:, sl] = (o[gi * BLOCK:(gi + 1) * BLOCK] * gate[:, sl]).astype(BF16)

    y = jnp.dot(v_ref[...], wout_ref[...], preferred_element_type=F32)
    o_ref[0] = _res_ln(x, y, g2_ref[...], b2_ref[...])


def _attn_layer(x, w_in, sink, rel_bias, w_out, g2, b2):
    bsz, seq, d = x.shape
    nb = seq // BLOCK
    w_bf = w_in.astype(BF16)
    qkv = _in_proj(x.reshape(bsz * seq, d), w_bf[:, :D_INNER + 2 * KV_WIDTH], BF16).reshape(bsz, seq, -1)
    q_pos = jnp.arange(BLOCK)
    k_off = jnp.arange(3 * BLOCK) - BLOCK
    bucket = _t5_bucket(k_off[None, :] - q_pos[:, None])
    k_col = D_INNER // KV_WIDTH
    v_col = k_col + 1
    prev = lambda i, n: jnp.maximum(n - 1, 0)
    nxt = lambda i, n: jnp.minimum(n + 1, nb - 1)
    kv = lambda colb, f: pl.BlockSpec((1, BLOCK, KV_WIDTH), lambda i, n: (i, f(i, n), colb))
    cur = lambda i, n: n
    return pl.pallas_call(
        _attn_layer_kernel,
        grid=(bsz, nb),
        in_specs=[
            pl.BlockSpec((1, BLOCK, D_INNER), lambda i, n: (i, n, 0)),
            kv(k_col, prev), kv(k_col, cur), kv(k_col, nxt),
            kv(v_col, prev), kv(v_col, cur), kv(v_col, nxt),
            pl.BlockSpec((1, BLOCK, d), lambda i, n: (i, n, 0)),
            _const_spec((BLOCK, 3 * BLOCK)),
            pl.BlockSpec(memory_space=pltpu.SMEM),
            pl.BlockSpec(memory_space=pltpu.SMEM),
            _const_spec((d, D_INNER)),
            _const_spec((D_INNER, d)),
            _const_spec((1, d)),
            _const_spec((1, d)),
        ],
        out_specs=pl.BlockSpec((1, BLOCK, d), lambda i, n: (i, n, 0)),
        out_shape=jax.ShapeDtypeStruct((bsz, seq, d), F32),
        scratch_shapes=[pltpu.VMEM((N_HEADS, BLOCK, 3 * BLOCK), F32),
                        pltpu.VMEM((BLOCK, D_INNER), BF16)],
        compiler_params=pltpu.CompilerParams(dimension_semantics=("arbitrary", "arbitrary"),
                                             vmem_limit_bytes=VMEM_LIMIT),
        name="attn_layer",
    )(qkv, qkv, qkv, qkv, qkv, qkv, qkv, x, bucket, rel_bias, sink, w_bf[:, D_INNER + 2 * KV_WIDTH:],
      w_out.astype(BF16), g2.reshape(1, -1), b2.reshape(1, -1))


def _trunk(x, ln_g, ln_b, w_in_conv, conv_w, conv_b, conv_ln_g, conv_ln_b, w_out_conv,
           w_in_hgrn, hgrn_lb, hgrn_norm_g, w_out_hgrn, w_in_attn, attn_sink, rel_bias, w_out_attn):
    for i in range(DEPTH):
        j = i // N_MIXERS
        kind = i % N_MIXERS
        if kind == 0:
            x = _conv_layer(x, w_in_conv[j], conv_w[j], conv_b[j], conv_ln_g[j], conv_ln_b[j],
                            w_out_conv[j], ln_g[i], ln_b[i])
        elif kind == 1:
            x = _hgrn_layer(x, w_in_hgrn[j], hgrn_lb, i, hgrn_norm_g[j], w_out_hgrn[j],
                            ln_g[i], ln_b[i])
        else:
            x = _attn_layer(x, w_in_attn[j], attn_sink[j], rel_bias, w_out_attn[j], ln_g[i], ln_b[i])
    return x


def kernel(x_prompt, x_sample, ln_g, ln_b, w_in_conv, conv_w, conv_b, conv_ln_g, conv_ln_b,
           w_out_conv, w_in_hgrn, hgrn_lb, hgrn_norm_g, w_out_hgrn, w_in_attn, attn_sink,
           rel_bias, w_out_attn):
    params = (ln_g, ln_b, w_in_conv, conv_w, conv_b, conv_ln_g, conv_ln_b, w_out_conv,
              w_in_hgrn, hgrn_lb, hgrn_norm_g, w_out_hgrn, w_in_attn, attn_sink, rel_bias, w_out_attn)
    return (_trunk(x_prompt, *params), _trunk(x_sample, *params))
```

```python
import functools
import math

import jax
import jax.numpy as jnp
import numpy as np
from jax import lax
from jax.experimental import pallas as pl
from jax.experimental.pallas import tpu as pltpu

D_MODEL = 1024
DEPTH = 4
N_MIXERS = 3
D_INNER = 2 * D_MODEL
CONV_WIDTH = 31
CONV_HALF = CONV_WIDTH // 2
HEAD_DIM = 128
N_HEADS = D_INNER // HEAD_DIM
KV_HEADS = 4
GROUP = N_HEADS // KV_HEADS
KV_WIDTH = KV_HEADS * HEAD_DIM
ATTN_SCALE = HEAD_DIM ** -0.5
WINDOW = 128
BLOCK = 128
REL_BUCKETS = 32
REL_MAX_DIST = 128
ALPHA = (2 * DEPTH) ** 0.25
LN_EPS = 1e-5

LANES = 128
SUBLANES = 8
VMEM_LIMIT = 56 * 1024 * 1024

CONV_TL = 512
CONV_HALO = 16
CONV_RC = 128
SCAN_C = 128
SCAN_LEVELS = SCAN_C.bit_length() - 1
SCAN_HB = 2
SCAN_CH = 8
SCAN_PHASES = 6
PROJ_TM = 2048
PROJ_TN = 1024
OUT_TM = 256
F_FLOOR = 1e-37

F32 = jnp.float32
BF16 = jnp.bfloat16


def _const_spec(shape):
    return pl.BlockSpec(shape, lambda *_: (0,) * len(shape), pipeline_mode=pl.Buffered(1))


def _sigmoid(x):
    return 1.0 / (1.0 + jnp.exp(-x))


def _silu(x):
    return x * _sigmoid(x)


def _res_ln(x, y, g, b):
    z = ALPHA * x + y
    mu = jnp.mean(z, axis=-1, keepdims=True)
    d = z - mu
    var = jnp.mean(d * d, axis=-1, keepdims=True)
    return d * lax.rsqrt(var + LN_EPS) * g + b


def _proj_kernel(x_ref, w_ref, o_ref):
    o_ref[...] = jnp.dot(x_ref[...].astype(BF16), w_ref[...],
                         preferred_element_type=F32).astype(o_ref.dtype)


def _in_proj(x2d, w_bf, out_dtype):
    t, d = x2d.shape
    n = w_bf.shape[1]
    return pl.pallas_call(
        _proj_kernel,
        grid=(t // PROJ_TM, n // PROJ_TN),
        in_specs=[pl.BlockSpec((PROJ_TM, d), lambda i, j: (i, 0)),
                  pl.BlockSpec((d, PROJ_TN), lambda i, j: (0, j))],
        out_specs=pl.BlockSpec((PROJ_TM, PROJ_TN), lambda i, j: (i, j)),
        out_shape=jax.ShapeDtypeStruct((t, n), out_dtype),
        compiler_params=pltpu.CompilerParams(dimension_semantics=("arbitrary", "arbitrary"),
                                             vmem_limit_bytes=VMEM_LIMIT),
        name="in_proj",
    )(x2d, w_bf)


def _proj_pair_kernel(x_ref, w_ref, o_ref):
    acc = jnp.dot(x_ref[...].astype(BF16), w_ref[...], preferred_element_type=F32)
    half = acc.shape[1] // 2
    lo = lax.bitcast_convert_type(acc[:, :half].astype(BF16).astype(F32), jnp.uint32) >> 16
    hi = lax.bitcast_convert_type(acc[:, half:].astype(BF16).astype(F32), jnp.uint32)
    o_ref[...] = lo | hi


def _in_proj_pairs(x2d, w_bf):
    t, d = x2d.shape
    n = w_bf.shape[1]
    return pl.pallas_call(
        _proj_pair_kernel,
        grid=(t // PROJ_TM, n // PROJ_TN),
        in_specs=[pl.BlockSpec((PROJ_TM, d), lambda i, j: (i, 0)),
                  pl.BlockSpec((d, PROJ_TN), lambda i, j: (0, j))],
        out_specs=pl.BlockSpec((PROJ_TM, PROJ_TN // 2), lambda i, j: (i, j)),
        out_shape=jax.ShapeDtypeStruct((t, n // 2), jnp.uint32),
        compiler_params=pltpu.CompilerParams(dimension_semantics=("arbitrary", "arbitrary"),
                                             vmem_limit_bytes=VMEM_LIMIT),
        name="in_proj_pairs",
    )(x2d, w_bf)


def _conv_stage_a(xe_ref, w3_ref, jj, prev_ok, next_ok, uext_ref, sg_ref):
    tl, halo, pair = CONV_TL, CONV_HALO, 2 * LANES
    h = jnp.dot(xe_ref[...], w3_ref[jj], preferred_element_type=F32)
    u = h[:, :pair] * _sigmoid(h[:, pair:2 * pair])
    u = jnp.concatenate([jnp.where(prev_ok, u[:halo], 0.0), u[halo:halo + tl],
                         jnp.where(next_ok, u[halo + tl:], 0.0)], axis=0)
    sg = _silu(h[halo:halo + tl, 2 * pair:])
    for i in range(2):
        uext_ref[2 * jj + i] = u[:, i * LANES:(i + 1) * LANES]
        sg_ref[2 * jj + i] = sg[:, i * LANES:(i + 1) * LANES]


def _conv_layer_kernel(nt, xpn_ref, xcn_ref, xnn_ref, xc_ref, w3_ref, cw_ref, cb_ref, lng_ref, lnb_ref,
                       wout_ref, g2_ref, b2_ref, o_ref, xe_ref, uext0_ref, uext1_ref, sg0_ref, sg1_ref,
                       conv_ref, v_ref):
    tl, halo = CONV_TL, CONV_HALO
    n_lane_blocks = D_INNER // LANES
    s = pl.program_id(0)

    @pl.when(s == 0)
    def _():
        xe_ref[...] = jnp.concatenate([jnp.zeros((halo, D_MODEL), F32), xc_ref[0], xcn_ref[0, :halo]],
                                      axis=0).astype(BF16)

        def first(jj, carry):
            _conv_stage_a(xe_ref, w3_ref, jj, False, True, uext0_ref, sg0_ref)
            return carry

        lax.fori_loop(0, n_lane_blocks // 2, first, 0)

    t1 = jnp.minimum(s + 1, pl.num_programs(0) - 1) % nt
    prev_ok = t1 > 0
    next_ok = t1 < nt - 1

    def step(uext_ref, sg_ref, uext_next_ref, sg_next_ref):
        def lane_block(j, carry):
            w = cw_ref[j]
            bias = cb_ref[j]
            for rc in range(tl // CONV_RC):
                acc = jnp.broadcast_to(bias, (CONV_RC, LANES))
                for k in range(CONV_WIDTH):
                    start = rc * CONV_RC + k + halo - CONV_HALF
                    acc = acc + w[k:k + 1, :] * uext_ref[j, pl.ds(start, CONV_RC), :]
                conv_ref[j, pl.ds(rc * CONV_RC, CONV_RC), :] = acc
            return carry

        lax.fori_loop(0, n_lane_blocks, lane_block, 0)

        xe_ref[...] = jnp.concatenate([xpn_ref[0], xcn_ref[0], xnn_ref[0]], axis=0).astype(BF16)
        for jj in range(n_lane_blocks // 2):
            _conv_stage_a(xe_ref, w3_ref, jj, prev_ok, next_ok, uext_next_ref, sg_next_ref)

        s1 = conv_ref[0]
        for j in range(1, n_lane_blocks):
            s1 = s1 + conv_ref[j]
        mu = jnp.sum(s1, axis=-1, keepdims=True) * (1.0 / D_INNER)
        s2 = jnp.zeros((tl, LANES), F32)
        for j in range(n_lane_blocks):
            d = conv_ref[j] - mu
            s2 = s2 + d * d
        rstd = lax.rsqrt(jnp.sum(s2, axis=-1, keepdims=True) * (1.0 / D_INNER) + LN_EPS)
        for j in range(n_lane_blocks):
            sl = slice(j * LANES, (j + 1) * LANES)
            c = (conv_ref[j] - mu) * rstd * lng_ref[:, sl] + lnb_ref[:, sl]
            v_ref[:, sl] = (_silu(c) * sg_ref[j]).astype(BF16)

        y = jnp.dot(v_ref[...], wout_ref[...], preferred_element_type=F32)
        o_ref[0] = _res_ln(xc_ref[0], y, g2_ref[...], b2_ref[...])

    @pl.when(s % 2 == 0)
    def _():
        step(uext0_ref, sg0_ref, uext1_ref, sg1_ref)

    @pl.when(s % 2 == 1)
    def _():
        step(uext1_ref, sg1_ref, uext0_ref, sg0_ref)


def _conv_layer(x, w_in, conv_w, conv_b, ln_g, ln_b, w_out, g2, b2):
    bsz, seq, d = x.shape
    tl, halo = CONV_TL, CONV_HALO
    n_lb = D_INNER // LANES
    n_pair = n_lb // 2
    nt = seq // tl
    assert nt > 1
    steps = bsz * nt
    a, b, g = jnp.split(w_in.astype(BF16), 3, axis=-1)
    w3 = jnp.concatenate([p.reshape(d, n_pair, 2 * LANES) for p in (a, b, g)], axis=-1)
    w3 = w3.transpose(1, 0, 2)
    cw = jnp.pad(conv_w, ((0, 1), (0, 0))).reshape(CONV_WIDTH + 1, n_lb, LANES).transpose(1, 0, 2)
    cb = conv_b.reshape(n_lb, 1, LANES)
    hpt = tl // halo
    last_halo = seq // halo - 1

    def nxt(s):
        s1 = jnp.minimum(s + 1, steps - 1)
        return s1 // nt, s1 % nt

    def prev_halo(s):
        b1, t1 = nxt(s)
        return b1, jnp.maximum(t1 * hpt - 1, 0), 0

    def next_halo(s):
        b1, t1 = nxt(s)
        return b1, jnp.minimum((t1 + 1) * hpt, last_halo), 0

    return pl.pallas_call(
        functools.partial(_conv_layer_kernel, nt),
        grid=(steps,),
        in_specs=[
            pl.BlockSpec((1, halo, d), prev_halo),
            pl.BlockSpec((1, tl, d), lambda s: (*nxt(s), 0)),
            pl.BlockSpec((1, halo, d), next_halo),
            pl.BlockSpec((1, tl, d), lambda s: (s // nt, s % nt, 0)),
            _const_spec((n_pair, d, 6 * LANES)),
            _const_spec((n_lb, CONV_WIDTH + 1, LANES)),
            _const_spec((n_lb, 1, LANES)),
            _const_spec((1, D_INNER)),
            _const_spec((1, D_INNER)),
            _const_spec((D_INNER, d)),
            _const_spec((1, d)),
            _const_spec((1, d)),
        ],
        out_specs=pl.BlockSpec((1, tl, d), lambda s: (s // nt, s % nt, 0)),
        out_shape=jax.ShapeDtypeStruct((bsz, seq, d), F32),
        scratch_shapes=[
            pltpu.VMEM((tl + 2 * halo, d), BF16),
            pltpu.VMEM((n_lb, tl + 2 * halo, LANES), F32),
            pltpu.VMEM((n_lb, tl + 2 * halo, LANES), F32),
            pltpu.VMEM((n_lb, tl, LANES), F32),
            pltpu.VMEM((n_lb, tl, LANES), F32),
            pltpu.VMEM((n_lb, tl, LANES), F32),
            pltpu.VMEM((tl, D_INNER), BF16),
        ],
        compiler_params=pltpu.CompilerParams(dimension_semantics=("arbitrary",),
                                             vmem_limit_bytes=VMEM_LIMIT),
        name="conv_layer",
    )(x, x, x, x, w3, cw, cb, ln_g.reshape(1, -1), ln_b.reshape(1, -1), w_out.astype(BF16),
      g2.reshape(1, -1), b2.reshape(1, -1))


def _scan_masks():
    half, nb8 = SCAN_C // 2, SCAN_C // SUBLANES
    i = np.arange(half)[:, None]
    j = np.arange(half)[None, :]
    same8 = (i % nb8) == (j % nb8)
    masks = [same8 & (i // (nb8 << lvl) == j // (nb8 << lvl)) for lvl in range(3)]
    masks += [i // (SUBLANES << lvl) == j // (SUBLANES << lvl) for lvl in range(SCAN_LEVELS - 3)]
    return np.stack(masks).astype(np.float32)


def _gates(raw, lb):
    e_abs = jnp.exp(-jnp.abs(raw))
    r_abs = 1.0 / (1.0 + e_abs)
    er = e_abs * r_abs
    pos = raw >= 0
    c1 = 1.0 - lb
    k = c1 * jnp.where(pos, er, r_abs)
    f = lb + c1 * jnp.where(pos, r_abs, er)
    return k, jnp.log(jnp.maximum(f, F_FLOOR))


_NT = (((1,), (1,)), ((), ()))


def _level_scores(qc, kc, mask):
    s = lax.dot_general(qc.astype(BF16), kc.astype(BF16), _NT, preferred_element_type=F32)
    if mask is not None:
        s = s * mask
    return s.astype(BF16)


def _scan_chunk(q_ref, f_ref, v_ref, half, row0, lb, st_ref, tmp_ref, masks_ref, reverse):
    c, sub = SCAN_C, SUBLANES
    nb8 = c // sub

    def unpack(words):
        bits = (words << 16) if half == 0 else (words & jnp.uint32(0xFFFF0000))
        return lax.bitcast_convert_type(bits, F32)

    def piece(ref, r):
        return unpack(ref[0, pl.ds(row0 + r, nb8, stride=sub), :])

    q = [_silu(piece(q_ref, r)) for r in range(sub)]
    kp = [_gates(piece(f_ref, r), lb) for r in range(sub)]
    k = [a for a, _ in kp]
    p = [b for _, b in kp]
    v = [piece(v_ref, r) for r in range(sub)]
    o = [jnp.sum(q[r] * k[r], axis=-1, keepdims=True) * v[r] for r in range(sub)]
    yield None
    low = []
    for lvl in range(3):
        m = 1 << lvl
        q_rows, k_rows, e_of = [], [], {}
        for blk in range(0, sub, 2 * m):
            lo, hi = list(range(blk, blk + m)), list(range(blk + m, blk + 2 * m))
            q_rows += lo if reverse else hi
            k_rows += hi if reverse else lo
            for r in lo + hi:
                e_of[r] = blk + m if reverse else blk + m - 1
        qc = jnp.concatenate([q[r] * jnp.exp(p[r]) for r in q_rows], axis=0)
        kc = jnp.concatenate([k[r] if r == e_of[r] else k[r] * jnp.exp(p[e_of[r]] - p[r])
                              for r in k_rows], axis=0)
        vc = jnp.concatenate([v[r] for r in k_rows], axis=0)
        low.append((q_rows, _level_scores(qc, kc, masks_ref[lvl]), vc.astype(BF16)))
        for r in q_rows:
            p[r] = p[r] + p[e_of[r]]
    for r in range(sub):
        for slot, arr in enumerate((p, q, k)):
            tmp_ref[slot, pl.ds(r, nb8, stride=sub), :] = arr[r]
    yield None
    low = [(q_rows, jnp.dot(sc, vc, preferred_element_type=F32)) for q_rows, sc, vc in low]
    yield None
    for q_rows, ol in low:
        for i, r in enumerate(q_rows):
            o[r] = o[r] + ol[i * nb8:(i + 1) * nb8]
    for r in range(sub):
        tmp_ref[3, pl.ds(r, nb8, stride=sub), :] = o[r]
    tile = lambda x, i: x[i * sub:(i + 1) * sub]
    pn, qn, kn, vn = tmp_ref[0], tmp_ref[1], tmp_ref[2], unpack(v_ref[0, row0:row0 + c, :])
    pt = [tile(pn, i) for i in range(nb8)]
    qt = [tile(qn, i) for i in range(nb8)]
    kt = [tile(kn, i) for i in range(nb8)]
    vt = [tile(vn, i) for i in range(nb8)]
    high = []
    for lvl in range(3, SCAN_LEVELS):
        mt = (1 << lvl) // sub
        q_tiles, k_tiles, eb = [], [], {}
        for blk in range(0, nb8, 2 * mt):
            lo, hi = list(range(blk, blk + mt)), list(range(blk + mt, blk + 2 * mt))
            q_tiles += lo if reverse else hi
            k_tiles += hi if reverse else lo
            e = pt[blk + mt][0:1, :] if reverse else pt[blk + mt - 1][sub - 1:sub, :]
            e = jnp.broadcast_to(e, (sub, LANES))
            for i in lo + hi:
                eb[i] = e
        qc = jnp.concatenate([qt[i] * jnp.exp(pt[i]) for i in q_tiles], axis=0)
        kc = jnp.concatenate([kt[i] * jnp.exp(eb[i] - pt[i]) for i in k_tiles], axis=0)
        vc = jnp.concatenate([vt[i] for i in k_tiles], axis=0)
        high.append((q_tiles, _level_scores(qc, kc, masks_ref[lvl] if lvl < SCAN_LEVELS - 1 else None),
                     vc.astype(BF16)))
        for i in q_tiles:
            pt[i] = pt[i] + eb[i]
    yield None
    pn = jnp.concatenate(pt, axis=0)
    edge = pt[0][0:1, :] if reverse else pt[nb8 - 1][sub - 1:sub, :]
    st = st_ref[...]
    o_inter = lax.dot_general((qn * jnp.exp(pn)).astype(BF16), st.astype(BF16), _NT,
                              preferred_element_type=F32)
    k_dec = (kn * jnp.exp(edge - pn)).astype(BF16)
    st_ref[...] = jnp.exp(edge) * st + lax.dot_general(
        vn.astype(BF16), k_dec, (((0,), (0,)), ((), ())), preferred_element_type=F32)
    high = [(q_tiles, jnp.dot(sc, vc, preferred_element_type=F32)) for q_tiles, sc, vc in high]
    yield None
    on = tmp_ref[3]
    ot = [tile(on, i) for i in range(nb8)]
    for q_tiles, ol in high:
        for n, i in enumerate(q_tiles):
            ot[i] = ot[i] + ol[n * sub:(n + 1) * sub]
    yield jnp.concatenate(ot, axis=0) + o_inter


def _hgrn_scan_kernel(layer, *refs):
    hb = SCAN_HB
    in_refs = refs[:6]
    lb_ref, masks_ref, of_ref, ob_ref, st_ref = refs[6:11]
    tmp_refs = refs[11:]

    @pl.when(pl.program_id(2) == 0)
    def _():
        st_ref[...] = jnp.zeros_like(st_ref)

    z = lb_ref[...]
    ez = jnp.exp(z - jnp.max(z, axis=1, keepdims=True))
    pz = ez / jnp.sum(ez, axis=1, keepdims=True)
    lb = jnp.sum(pz[:, :layer + 1, :], axis=1) - pz[:, 0, :]

    streams = []
    for step in range(SCAN_CH):
        for d, o_ref in enumerate((of_ref, ob_ref)):
            ci = step if d == 0 else SCAN_CH - 1 - step
            for j in range(hb):
                sl = slice(j * LANES, (j + 1) * LANES)
                q_ref, f_ref, v_ref = in_refs[3 * d:3 * d + 3]
                gen = _scan_chunk(q_ref, f_ref, v_ref, j, ci * SCAN_C, lb[d:d + 1, sl], st_ref.at[d, j],
                                  tmp_refs[len(streams)], masks_ref, d == 1)
                streams.append((o_ref, ci * SCAN_C, sl, gen))
    outs = [None] * len(streams)
    for _ in range(SCAN_PHASES):
        for n, stream in enumerate(streams):
            outs[n] = next(stream[3])
    for (o_ref, row0, sl, _), out in zip(streams, outs):
        o_ref[0, row0:row0 + SCAN_C, sl] = out


def _hgrn_scan(h, hgrn_lb, layer):
    bsz, seq, _ = h.shape
    c, hb = SCAN_C * SCAN_CH, SCAN_HB
    assert hb == 2
    nc = seq // c
    n_pairs = N_HEADS // hb

    def col(part, rev):
        if rev:
            return pl.BlockSpec((1, c, LANES), lambda i, hg, n: (i, nc - 1 - n, part * n_pairs + hg))
        return pl.BlockSpec((1, c, LANES), lambda i, hg, n: (i, n, part * n_pairs + hg))

    in_specs = [col(0, False), col(1, False), col(3, False), col(0, True), col(2, True), col(3, True)]
    in_specs += [pl.BlockSpec((2, DEPTH, hb * LANES), lambda i, hg, n: (0, 0, hg)),
                 _const_spec((SCAN_LEVELS, SCAN_C // 2, SCAN_C // 2))]
    out_f = pl.BlockSpec((1, c, hb * LANES), lambda i, hg, n: (i, n, hg))
    out_b = pl.BlockSpec((1, c, hb * LANES), lambda i, hg, n: (i, nc - 1 - n, hg))
    return pl.pallas_call(
        functools.partial(_hgrn_scan_kernel, layer),
        grid=(bsz, N_HEADS // hb, nc),
        in_specs=in_specs,
        out_specs=[out_f, out_b],
        out_shape=[jax.ShapeDtypeStruct((bsz, seq, D_INNER), F32)] * 2,
        scratch_shapes=[pltpu.VMEM((2, hb, HEAD_DIM, HEAD_DIM), F32)]
        + [pltpu.VMEM((4, SCAN_C, LANES), F32)] * (2 * hb * SCAN_CH),
        compiler_params=pltpu.CompilerParams(
            dimension_semantics=("arbitrary", "arbitrary", "arbitrary"), vmem_limit_bytes=VMEM_LIMIT),
        name="hgrn_scan",
    )(h, h, h, h, h, h, hgrn_lb, jnp.asarray(_scan_masks()))


def _hgrn_out_kernel(of_ref, ob_ref, x_ref, wg_ref, ng_ref, wout_ref, g2_ref, b2_ref, o_ref, v_ref):
    x = x_ref[...]
    gate = _silu(jnp.dot(x.astype(BF16), wg_ref[...], preferred_element_type=F32))
    for hd in range(N_HEADS):
        sl = slice(hd * HEAD_DIM, (hd + 1) * HEAD_DIM)
        o = of_ref[:, sl] + ob_ref[:, sl]
        o = o * lax.rsqrt(jnp.mean(o * o, axis=-1, keepdims=True) + LN_EPS)
        v_ref[:, sl] = (o * ng_ref[:, sl] * gate[:, sl]).astype(BF16)
    y = jnp.dot(v_ref[...], wout_ref[...], preferred_element_type=F32)
    o_ref[...] = _res_ln(x, y, g2_ref[...], b2_ref[...])


def _hgrn_out(o_fw, o_bw, x2d, w_gate, norm_g, w_out, g2, b2):
    t, d = x2d.shape
    tm = OUT_TM
    row = lambda i: (i, 0)
    return pl.pallas_call(
        _hgrn_out_kernel,
        grid=(t // tm,),
        in_specs=[pl.BlockSpec((tm, D_INNER), row), pl.BlockSpec((tm, D_INNER), row),
                  pl.BlockSpec((tm, d), row), _const_spec((d, D_INNER)),
                  _const_spec((1, D_INNER)), _const_spec((D_INNER, d)),
                  _const_spec((1, d)), _const_spec((1, d))],
        out_specs=pl.BlockSpec((tm, d), row),
        out_shape=jax.ShapeDtypeStruct((t, d), F32),
        scratch_shapes=[pltpu.VMEM((tm, D_INNER), BF16)],
        compiler_params=pltpu.CompilerParams(dimension_semantics=("arbitrary",),
                                             vmem_limit_bytes=VMEM_LIMIT),
        name="hgrn_out",
    )(o_fw, o_bw, x2d, w_gate, norm_g.reshape(1, -1), w_out.astype(BF16), g2.reshape(1, -1),
      b2.reshape(1, -1))


def _hgrn_layer(x, w_in, hgrn_lb, layer, norm_g, w_out, g2, b2):
    bsz, seq, d = x.shape
    x2d = x.reshape(bsz * seq, d)
    w_bf = w_in.astype(BF16)
    per_tile = PROJ_TN // (2 * HEAD_DIM)
    w4 = w_bf[:, :4 * D_INNER].reshape(d, 4, N_HEADS // (2 * per_tile), per_tile, 2, HEAD_DIM)
    w4 = w4.transpose(0, 1, 2, 4, 3, 5).reshape(d, 4 * D_INNER)
    h = _in_proj_pairs(x2d, w4)
    o_fw, o_bw = _hgrn_scan(h.reshape(bsz, seq, -1), hgrn_lb, layer)
    y = _hgrn_out(o_fw.reshape(bsz * seq, -1), o_bw.reshape(bsz * seq, -1), x2d, w_bf[:, 4 * D_INNER:],
                  norm_g, w_out, g2, b2)
    return y.reshape(bsz, seq, d)


def _t5_bucket(rel):
    nb = REL_BUCKETS // 2
    max_exact = nb // 2
    ret = (rel > 0).astype(jnp.int32) * nb
    n = jnp.abs(rel)
    large = max_exact + (jnp.log(jnp.maximum(n, 1).astype(jnp.float32) / max_exact)
                         / math.log(REL_MAX_DIST / max_exact) * (nb - max_exact)).astype(jnp.int32)
    large = jnp.minimum(large, nb - 1)
    return ret + jnp.where(n < max_exact, n, large)


def _attn_layer_kernel(q_ref, kp_ref, kc_ref, kn_ref, vp_ref, vc_ref, vn_ref, x_ref,
                       bucket_ref, relb_ref, sink_ref, wg_ref, wout_ref, g2_ref, b2_ref, o_ref,
                       bias_ref, v_ref):
    n = pl.program_id(1)
    inv_scale = 1.0 / ATTN_SCALE
    neg_inf = -jnp.inf

    @pl.when((pl.program_id(0) == 0) & (n == 0))
    def _():
        bucket = bucket_ref[...]
        col = lax.broadcasted_iota(jnp.int32, bucket.shape, 1)
        row = lax.broadcasted_iota(jnp.int32, bucket.shape, 0)
        in_band = jnp.abs(col - BLOCK - row) <= WINDOW
        for h in range(N_HEADS):
            acc = jnp.zeros(bucket.shape, F32)
            for b in range(REL_BUCKETS):
                acc = jnp.where(bucket == b, relb_ref[b, h], acc)
            bias_ref[h] = jnp.where(in_band, acc * inv_scale, neg_inf)

    x = x_ref[0]
    gate = _silu(jnp.dot(x.astype(BF16), wg_ref[...], preferred_element_type=F32))
    kwin = jnp.concatenate([kp_ref[0], kc_ref[0], kn_ref[0]], axis=0)
    vwin = jnp.concatenate([vp_ref[0], vc_ref[0], vn_ref[0]], axis=0)
    qb = q_ref[0]
    rows = GROUP * BLOCK
    has_prev = n > 0
    has_next = n < pl.num_programs(1) - 1
    head_of_row = lax.broadcasted_iota(jnp.int32, (rows, 1), 0) // BLOCK

    scores = []
    for kh in range(KV_HEADS):
        h0 = kh * GROUP
        qg = jnp.concatenate([qb[:, (h0 + gi) * HEAD_DIM:(h0 + gi + 1) * HEAD_DIM]
                              for gi in range(GROUP)], axis=0)
        scores.append(lax.dot_general(qg, kwin[:, kh * HEAD_DIM:(kh + 1) * HEAD_DIM], _NT,
                                      preferred_element_type=F32))
    c2 = ATTN_SCALE * math.log2(math.e)
    for kh in range(KV_HEADS):
        h0 = kh * GROUP
        s = scores[kh] + bias_ref[pl.ds(h0, GROUP)].reshape(rows, 3 * BLOCK)
        s = jnp.concatenate([jnp.where(has_prev, s[:, :BLOCK], neg_inf), s[:, BLOCK:2 * BLOCK],
                             jnp.where(has_next, s[:, 2 * BLOCK:], neg_inf)], axis=1)
        sink = jnp.zeros((rows, 1), F32)
        for gi in range(GROUP):
            sink = jnp.where(head_of_row == gi, sink_ref[h0 + gi] * inv_scale, sink)
        mx = jnp.maximum(jnp.max(s, axis=-1, keepdims=True), sink)
        p = jnp.exp2((s - mx) * c2)
        den = jnp.sum(p, axis=-1, keepdims=True) + jnp.exp2((sink - mx) * c2)
        o = jnp.dot(p.astype(BF16), vwin[:, kh * HEAD_DIM:(kh + 1) * HEAD_DIM],
                    preferred_element_type=F32) / den
        for gi in range(GROUP):
            sl = slice((h0 + gi) * HEAD_DIM, (h0 + gi + 1) * HEAD_DIM)
            v_ref[:, sl] = (o[gi * BLOCK:(gi + 1) * BLOCK] * gate[:, sl]).astype(BF16)

    y = jnp.dot(v_ref[...], wout_ref[...], preferred_element_type=F32)
    o_ref[0] = _res_ln(x, y, g2_ref[...], b2_ref[...])


def _attn_layer(x, w_in, sink, rel_bias, w_out, g2, b2):
    bsz, seq, d = x.shape
    nb = seq // BLOCK
    w_bf = w_in.astype(BF16)
    qkv = _in_proj(x.reshape(bsz * seq, d), w_bf[:, :D_INNER + 2 * KV_WIDTH], BF16).reshape(bsz, seq, -1)
    q_pos = jnp.arange(BLOCK)
    k_off = jnp.arange(3 * BLOCK) - BLOCK
    bucket = _t5_bucket(k_off[None, :] - q_pos[:, None])
    k_col = D_INNER // KV_WIDTH
    v_col = k_col + 1
    prev = lambda i, n: jnp.maximum(n - 1, 0)
    nxt = lambda i, n: jnp.minimum(n + 1, nb - 1)
    kv = lambda colb, f: pl.BlockSpec((1, BLOCK, KV_WIDTH), lambda i, n: (i, f(i, n), colb))
    cur = lambda i, n: n
    return pl.pallas_call(
        _attn_layer_kernel,
        grid=(bsz, nb),
        in_specs=[
            pl.BlockSpec((1, BLOCK, D_INNER), lambda i, n: (i, n, 0)),
            kv(k_col, prev), kv(k_col, cur), kv(k_col, nxt),
            kv(v_col, prev), kv(v_col, cur), kv(v_col, nxt),
            pl.BlockSpec((1, BLOCK, d), lambda i, n: (i, n, 0)),
            _const_spec((BLOCK, 3 * BLOCK)),
            pl.BlockSpec(memory_space=pltpu.SMEM),
            pl.BlockSpec(memory_space=pltpu.SMEM),
            _const_spec((d, D_INNER)),
            _const_spec((D_INNER, d)),
            _const_spec((1, d)),
            _const_spec((1, d)),
        ],
        out_specs=pl.BlockSpec((1, BLOCK, d), lambda i, n: (i, n, 0)),
        out_shape=jax.ShapeDtypeStruct((bsz, seq, d), F32),
        scratch_shapes=[pltpu.VMEM((N_HEADS, BLOCK, 3 * BLOCK), F32),
                        pltpu.VMEM((BLOCK, D_INNER), BF16)],
        compiler_params=pltpu.CompilerParams(dimension_semantics=("arbitrary", "arbitrary"),
                                             vmem_limit_bytes=VMEM_LIMIT),
        name="attn_layer",
    )(qkv, qkv, qkv, qkv, qkv, qkv, qkv, x, bucket, rel_bias, sink, w_bf[:, D_INNER + 2 * KV_WIDTH:],
      w_out.astype(BF16), g2.reshape(1, -1), b2.reshape(1, -1))


def _trunk(x, ln_g, ln_b, w_in_conv, conv_w, conv_b, conv_ln_g, conv_ln_b, w_out_conv,
           w_in_hgrn, hgrn_lb, hgrn_norm_g, w_out_hgrn, w_in_attn, attn_sink, rel_bias, w_out_attn):
    for i in range(DEPTH):
        j = i // N_MIXERS
        kind = i % N_MIXERS
        if kind == 0:
            x = _conv_layer(x, w_in_conv[j], conv_w[j], conv_b[j], conv_ln_g[j], conv_ln_b[j],
                            w_out_conv[j], ln_g[i], ln_b[i])
        elif kind == 1:
            x = _hgrn_layer(x, w_in_hgrn[j], hgrn_lb, i, hgrn_norm_g[j], w_out_hgrn[j],
                            ln_g[i], ln_b[i])
        else:
            x = _attn_layer(x, w_in_attn[j], attn_sink[j], rel_bias, w_out_attn[j], ln_g[i], ln_b[i])
    return x


def kernel(x_prompt, x_sample, ln_g, ln_b, w_in_conv, conv_w, conv_b, conv_ln_g, conv_ln_b,
           w_out_conv, w_in_hgrn, hgrn_lb, hgrn_norm_g, w_out_hgrn, w_in_attn, attn_sink,
           rel_bias, w_out_attn):
    params = (ln_g, ln_b, w_in_conv, conv_w, conv_b, conv_ln_g, conv_ln_b, w_out_conv,
              w_in_hgrn, hgrn_lb, hgrn_norm_g, w_out_hgrn, w_in_attn, attn_sink, rel_bias, w_out_attn)
    return (_trunk(x_prompt, *params), _trunk(x_sample, *params))
```
